```python
import math
import jax, jax.numpy as jnp
from jax import lax
import numpy as np

D_MODEL = 2048
BATCH = 1
SEQ = 16384
DEPTH = 1

SSM_GROUP = 16
SSM_WIDTH = D_MODEL // 2
SSM_GROUPS = SSM_WIDTH // SSM_GROUP
SSM_STATE = 64
DT_MIN = 0.001
DT_MAX = 0.1
N_HEADS = 16
QK_NOPE = 128
QK_ROPE = 64
V_HEAD = 128
Q_LORA = 512
KV_LORA = 512
ROPE_THETA = 10000.0
BLOCK_Q = 128
N_BRANCHES = 2
GATE_WIDTH = N_BRANCHES * D_MODEL
IN_WIDTH = SSM_WIDTH + Q_LORA + KV_LORA + QK_ROPE + GATE_WIDTH
D_FF = -(-8 * D_MODEL // (3 * 256)) * 256
DEEPNORM_ALPHA = (2.0 * DEPTH) ** 0.25
DEEPNORM_BETA = (8.0 * DEPTH) ** -0.25
LN_EPS = 1e-5
RMS_EPS = 1e-6

kernel_name = "hybrid_s5_mla_gated_deepnorm"


def layer_norm(x, g, b):
    x32 = x.astype(jnp.float32)
    mu = jnp.mean(x32, axis=-1, keepdims=True)
    var = jnp.mean(jnp.square(x32 - mu), axis=-1, keepdims=True)
    y = (x32 - mu) * lax.rsqrt(var + LN_EPS) * g.astype(jnp.float32) + b.astype(jnp.float32)
    return y.astype(x.dtype)


def rms_norm(x, g):
    x32 = x.astype(jnp.float32)
    y = x32 * lax.rsqrt(jnp.mean(jnp.square(x32), axis=-1, keepdims=True) + RMS_EPS)
    return (y * g.astype(jnp.float32)).astype(x.dtype)


def rope_tables(positions):
    inv_freq = 1.0 / (ROPE_THETA ** (jnp.arange(0, QK_ROPE, 2, dtype=jnp.float32) / QK_ROPE))
    ang = positions.astype(jnp.float32)[..., None] * inv_freq
    return jnp.cos(ang), jnp.sin(ang)


def apply_rope(t, cos, sin):
    t32 = t.astype(jnp.float32)
    t1, t2 = jnp.split(t32, 2, axis=-1)
    out = jnp.concatenate([t1 * cos - t2 * sin, t1 * sin + t2 * cos], axis=-1)
    return out.astype(t.dtype)


def s5_scan(u, lam_re, lam_im, log_dt, b_re, b_im, c_re, c_im, d_skip):
    f32 = jnp.float32
    bsz, seq, _ = u.shape
    u32 = u.astype(f32).reshape(bsz, seq, SSM_GROUPS, SSM_GROUP)
    lam_re = lam_re.astype(f32)
    lam_im = lam_im.astype(f32)
    dt = jnp.exp(log_dt.astype(f32))[:, None]
    mag = jnp.exp(lam_re * dt)
    ang = lam_im * dt
    abar_re = mag * jnp.cos(ang)
    abar_im = mag * jnp.sin(ang)
    den = jnp.square(lam_re) + jnp.square(lam_im)
    num_re = abar_re - 1.0
    coef_re = (num_re * lam_re + abar_im * lam_im) / den
    coef_im = (abar_im * lam_re - num_re * lam_im) / den
    b_re = b_re.astype(f32)
    b_im = b_im.astype(f32)
    bbar_re = coef_re[..., None] * b_re - coef_im[..., None] * b_im
    bbar_im = coef_re[..., None] * b_im + coef_im[..., None] * b_re
    bu_re = jnp.einsum('bsgp,gnp->bsgn', u32, bbar_re)
    bu_im = jnp.einsum('bsgp,gnp->bsgn', u32, bbar_im)
    a_re = jnp.broadcast_to(abar_re, bu_re.shape)
    a_im = jnp.broadcast_to(abar_im, bu_im.shape)

    def combine(left, right):
        ar1, ai1, br1, bi1 = left
        ar2, ai2, br2, bi2 = right
        ar = ar1 * ar2 - ai1 * ai2
        ai = ar1 * ai2 + ai1 * ar2
        br = ar2 * br1 - ai2 * bi1 + br2
        bi = ar2 * bi1 + ai2 * br1 + bi2
        return ar, ai, br, bi

    _, _, st_re, st_im = lax.associative_scan(combine, (a_re, a_im, bu_re, bu_im), axis=1)
    y = (jnp.einsum('bsgn,gpn->bsgp', st_re, c_re.astype(f32))
         - jnp.einsum('bsgn,gpn->bsgp', st_im, c_im.astype(f32))
         + d_skip.astype(f32) * u32)
    return y.reshape(bsz, seq, SSM_WIDTH).astype(u.dtype)


def mla_attention(q_nope, q_rope, k_nope, k_rope, v):
    bsz, seq = q_nope.shape[:2]
    n_blocks = seq // BLOCK_Q
    scale = 1.0 / math.sqrt(QK_NOPE + QK_ROPE)
    qn_b = q_nope.reshape(bsz, n_blocks, BLOCK_Q, N_HEADS, QK_NOPE).transpose(1, 0, 2, 3, 4)
    qr_b = q_rope.reshape(bsz, n_blocks, BLOCK_Q, N_HEADS, QK_ROPE).transpose(1, 0, 2, 3, 4)
    key_pos = jnp.arange(seq)

    def attend(args):
        blk, qn, qr = args
        s = (jnp.einsum('bqhd,bkhd->bhqk', qn, k_nope)
             + jnp.einsum('bqhr,bkr->bhqk', qr, k_rope)).astype(jnp.float32) * scale
        q_pos = blk * BLOCK_Q + jnp.arange(BLOCK_Q)
        mask = key_pos[None, :] <= q_pos[:, None]
        s = jnp.where(mask[None, None], s, -jnp.inf)
        p = jax.nn.softmax(s, axis=-1).astype(v.dtype)
        return jnp.einsum('bhqk,bkhd->bqhd', p, v)

    o = lax.map(attend, (jnp.arange(n_blocks), qn_b, qr_b))
    return o.transpose(1, 0, 2, 3, 4).reshape(bsz, seq, N_HEADS * V_HEAD)


def setup_inputs(seed: int = 0) -> dict:
    key = jax.random.key(seed)
    ks = jax.random.split(key, 26)
    f32 = jnp.float32
    L = DEPTH

    def nrm(k, shape, scale):
        return jax.random.normal(k, shape, f32) * scale

    x = jax.random.normal(ks[0], (BATCH, SEQ, D_MODEL), f32)
    offset = jax.random.randint(ks[1], (BATCH, 1), 0, 4096, dtype=jnp.int32)
    positions = offset + jnp.arange(SEQ, dtype=jnp.int32)[None, :]

    w_in = nrm(ks[2], (L, D_MODEL, IN_WIDTH), D_MODEL ** -0.5)
    ssm_lambda_re = -0.5 + nrm(ks[3], (L, SSM_GROUPS, SSM_STATE), 0.01)
    ssm_lambda_im = (math.pi * jnp.arange(SSM_STATE, dtype=f32))[None, None, :] + nrm(ks[4], (L, SSM_GROUPS, SSM_STATE), 0.01)
    ssm_log_dt = jax.random.uniform(ks[5], (L, SSM_GROUPS), f32, math.log(DT_MIN), math.log(DT_MAX))
    ssm_b_re = nrm(ks[6], (L, SSM_GROUPS, SSM_STATE, SSM_GROUP), (2.0 * SSM_GROUP) ** -0.5)
    ssm_b_im = nrm(ks[7], (L, SSM_GROUPS, SSM_STATE, SSM_GROUP), (2.0 * SSM_GROUP) ** -0.5)
    ssm_c_re = nrm(ks[8], (L, SSM_GROUPS, SSM_GROUP, SSM_STATE), (2.0 * SSM_STATE) ** -0.5)
    ssm_c_im = nrm(ks[9], (L, SSM_GROUPS, SSM_GROUP, SSM_STATE), (2.0 * SSM_STATE) ** -0.5)
    ssm_d = nrm(ks[10], (L, SSM_GROUPS, SSM_GROUP), 1.0)
    w_glu = nrm(ks[11], (L, SSM_WIDTH, 2 * D_MODEL), SSM_WIDTH ** -0.5)

    q_norm_g = 1.0 + nrm(ks[12], (L, Q_LORA), 0.02)
    w_uq = nrm(ks[13], (L, Q_LORA, N_HEADS * (QK_NOPE + QK_ROPE)), Q_LORA ** -0.5)
    kv_norm_g = 1.0 + nrm(ks[14], (L, KV_LORA), 0.02)
    w_ukv = nrm(ks[15], (L, KV_LORA, N_HEADS * (QK_NOPE + V_HEAD)), KV_LORA ** -0.5)

    w_out = nrm(ks[16], (L, D_MODEL, D_MODEL), DEEPNORM_BETA * D_MODEL ** -0.5)
    ln1_g = 1.0 + nrm(ks[17], (L, D_MODEL), 0.02)
    ln1_b = nrm(ks[18], (L, D_MODEL), 0.02)

    w_ffn_gate = nrm(ks[19], (L, D_MODEL, D_FF), D_MODEL ** -0.5)
    w_ffn_up = nrm(ks[20], (L, D_MODEL, D_FF), D_MODEL ** -0.5)
    w_ffn_down = nrm(ks[21], (L, D_FF, D_MODEL), DEEPNORM_BETA * D_FF ** -0.5)
    ln2_g = 1.0 + nrm(ks[22], (L, D_MODEL), 0.02)
    ln2_b = nrm(ks[23], (L, D_MODEL), 0.02)

    return {
        "x": x, "positions": positions, "w_in": w_in,
        "ssm_lambda_re": ssm_lambda_re, "ssm_lambda_im": ssm_lambda_im, "ssm_log_dt": ssm_log_dt,
        "ssm_b_re": ssm_b_re, "ssm_b_im": ssm_b_im, "ssm_c_re": ssm_c_re, "ssm_c_im": ssm_c_im,
        "ssm_d": ssm_d, "w_glu": w_glu,
        "q_norm_g": q_norm_g, "w_uq": w_uq, "kv_norm_g": kv_norm_g, "w_ukv": w_ukv,
        "w_out": w_out, "ln1_g": ln1_g, "ln1_b": ln1_b,
        "w_ffn_gate": w_ffn_gate, "w_ffn_up": w_ffn_up, "w_ffn_down": w_ffn_down,
        "ln2_g": ln2_g, "ln2_b": ln2_b,
    }


def reference(x, positions, w_in, ssm_lambda_re, ssm_lambda_im, ssm_log_dt, ssm_b_re, ssm_b_im,
              ssm_c_re, ssm_c_im, ssm_d, w_glu, q_norm_g, w_uq, kv_norm_g, w_ukv, w_out,
              ln1_g, ln1_b, w_ffn_gate, w_ffn_up, w_ffn_down, ln2_g, ln2_b):
    bsz, seq, _ = x.shape
    cos, sin = rope_tables(positions)
    cos_h, sin_h = cos[:, :, None, :], sin[:, :, None, :]
    split_at = [SSM_WIDTH, SSM_WIDTH + Q_LORA, SSM_WIDTH + Q_LORA + KV_LORA,
                SSM_WIDTH + Q_LORA + KV_LORA + QK_ROPE]
    h = x
    for l in range(DEPTH):
        z = h @ w_in[l]
        u_ssm, c_q, c_kv, k_rope_raw, gate_logits = jnp.split(z, split_at, axis=-1)

        y_ssm = s5_scan(u_ssm, ssm_lambda_re[l], ssm_lambda_im[l], ssm_log_dt[l], ssm_b_re[l],
                        ssm_b_im[l], ssm_c_re[l], ssm_c_im[l], ssm_d[l])
        glu_a, glu_b = jnp.split(jax.nn.gelu(y_ssm) @ w_glu[l], 2, axis=-1)
        ssm_out = glu_a * jax.nn.sigmoid(glu_b)

        q = (rms_norm(c_q, q_norm_g[l]) @ w_uq[l]).reshape(bsz, seq, N_HEADS, QK_NOPE + QK_ROPE)
        q_nope, q_rope = jnp.split(q, [QK_NOPE], axis=-1)
        q_rope = apply_rope(q_rope, cos_h, sin_h)
        kv = (rms_norm(c_kv, kv_norm_g[l]) @ w_ukv[l]).reshape(bsz, seq, N_HEADS, QK_NOPE + V_HEAD)
        k_nope, v = jnp.split(kv, [QK_NOPE], axis=-1)
        k_rope = apply_rope(k_rope_raw, cos, sin)
        mla_out = mla_attention(q_nope, q_rope, k_nope, k_rope, v)

        g_ssm, g_mla = jnp.split(jax.nn.sigmoid(gate_logits), 2, axis=-1)
        mix = (g_ssm * ssm_out + g_mla * mla_out) @ w_out[l]
        h = layer_norm(DEEPNORM_ALPHA * h + mix, ln1_g[l], ln1_b[l])

        ffn = (jax.nn.silu(h @ w_ffn_gate[l]) * (h @ w_ffn_up[l])) @ w_ffn_down[l]
        h = layer_norm(DEEPNORM_ALPHA * h + ffn, ln2_g[l], ln2_b[l])
    return h
```

```python
import functools
import math

import jax
import jax.numpy as jnp
from jax import lax
from jax.experimental import pallas as pl
from jax.experimental.pallas import tpu as pltpu

F32 = jnp.float32
BF16 = jnp.bfloat16

D_MODEL = 2048
SSM_GROUP = 16
SSM_WIDTH = D_MODEL // 2
SSM_GROUPS = SSM_WIDTH // SSM_GROUP
SSM_STATE = 64
N_HEADS = 16
QK_NOPE = 128
QK_ROPE = 64
V_HEAD = 128
Q_LORA = 512
KV_LORA = 512
ROPE_THETA = 10000.0
DEPTH = 1
DEEPNORM_ALPHA = (2.0 * DEPTH) ** 0.25
LN_EPS = 1e-5
RMS_EPS = 1e-6

LANES = 128
V7X_VMEM_BYTES = 64 * 1024 * 1024
VMEM_LIMIT = 56 * 1024 * 1024

HALF_ROPE = QK_ROPE // 2
QK_PAD = 256
SSM_T = 8
SLAB_GROUPS = LANES // SSM_GROUP
N_SLABS = SSM_WIDTH // LANES
SLAB_K = SSM_T * LANES
SLAB_STATE = SLAB_GROUPS * 2 * SSM_STATE
HALF_STATE = SLAB_STATE // 2


def _cparams(sem, vmem=VMEM_LIMIT):
    return pltpu.CompilerParams(dimension_semantics=sem, vmem_limit_bytes=vmem)


def _resident(shape, index_map):
    return pl.BlockSpec(shape, index_map, pipeline_mode=pl.Buffered(1))


def _ssm_prep_kernel(lre_ref, lim_ref, ldt_ref, btr_ref, bti_ref, cr_ref, ci_ref,
                     kt_ref, pm_ref, qm_ref, ap_ref):
    lre = lre_ref[...]
    lim = lim_ref[...]
    dt = jnp.exp(ldt_ref[...])
    xr = lre * dt
    xi = lim * dt

    def apow(k):
        mag = jnp.exp(xr * float(k))
        ang = xi * float(k)
        return mag * jnp.cos(ang), mag * jnp.sin(ang)

    ar, ai = apow(1)
    den = lre * lre + lim * lim
    nr = ar - 1.0
    coef_re = (nr * lre + ai * lim) / den
    coef_im = (ai * lre - nr * lim) / den
    btr = btr_ref[...]
    bti = bti_ref[...]
    bbr = coef_re * btr - coef_im * bti
    bbi = coef_re * bti + coef_im * btr
    cr = cr_ref[...]
    ci = ci_ref[...]

    dn = (((2,), (2,)), ((0,), (0,)))
    for k in range(SSM_T + 1):
        pr, pi = (jnp.ones_like(xr), jnp.zeros_like(xr)) if k == 0 else apow(k)
        if k < SSM_T:
            car = cr * pr - ci * pi
            cai = cr * pi + ci * pr
            kt_ref[k] = (lax.dot_general(car, bbr, dn, precision=lax.Precision.HIGHEST,
                                         preferred_element_type=F32)
                         - lax.dot_general(cai, bbi, dn, precision=lax.Precision.HIGHEST,
                                           preferred_element_type=F32))
            pm_ref[SSM_T - 1 - k, :, :, 0:SSM_STATE] = pr * bbr - pi * bbi
            pm_ref[SSM_T - 1 - k, :, :, SSM_STATE:] = pr * bbi + pi * bbr
        if k >= 1:
            car = cr * pr - ci * pi
            cai = cr * pi + ci * pr
            qm_ref[k - 1, :, :, 0:SSM_STATE] = car
            qm_ref[k - 1, :, :, SSM_STATE:] = -cai
    for j in range(1, SSM_T + 1):
        pr, pi = apow(SSM_T * j)
        ap_ref[j - 1, :, :, 0:SSM_STATE] = pr
        ap_ref[j - 1, :, :, SSM_STATE:] = pi


def _ssm_prep(lam_re, lam_im, log_dt, b_re, b_im, c_re, c_im):
    g, n, p = SSM_GROUPS, SSM_STATE, SSM_GROUP
    out_shape = (
        jax.ShapeDtypeStruct((SSM_T, g, p, p), F32),
        jax.ShapeDtypeStruct((SSM_T, g, p, 2 * n), F32),
        jax.ShapeDtypeStruct((SSM_T, g, p, 2 * n), F32),
        jax.ShapeDtypeStruct((SSM_T, g, 1, 2 * n), F32),
    )
    return pl.pallas_call(
        _ssm_prep_kernel, out_shape=out_shape, name="ssm_prep",
        compiler_params=_cparams(None),
    )(lam_re.reshape(g, 1, n), lam_im.reshape(g, 1, n), log_dt.reshape(g, 1, 1),
      jnp.swapaxes(b_re, 1, 2), jnp.swapaxes(b_im, 1, 2), c_re, c_im)


def _ssm_operators(kt, pm, qm, ap, d_skip):
    s, gl, t, p, n = N_SLABS, SLAB_GROUPS, SSM_T, SSM_GROUP, SSM_STATE
    eye = jnp.eye(gl, dtype=F32)
    tt = jnp.arange(t)
    lag = tt[None, :] - tt[:, None]
    ktl = jnp.where((lag >= 0)[:, :, None, None, None], kt[jnp.clip(lag, 0, t - 1)], 0.0)
    ktl = ktl.reshape(t, t, s, gl, p, p)
    ktl = jnp.transpose(ktl, (2, 0, 3, 5, 1, 4))
    m_op = ktl[:, :, :, :, :, None, :] * eye[None, None, :, None, None, :, None]
    m_op = m_op.reshape(s, SLAB_K, SLAB_K).astype(BF16)

    pml = pm.reshape(t, s, gl, p, 2, n)
    pml = jnp.transpose(pml, (1, 0, 2, 3, 4, 5))
    p_op = pml[:, :, :, :, :, None, :] * eye[None, None, :, None, None, :, None]
    p_op = p_op.reshape(s, SLAB_K, SLAB_STATE).astype(BF16)

    qml = qm.reshape(t, s, gl, p, 2, n)
    qml = jnp.transpose(qml, (1, 4, 2, 5, 0, 3))
    q_op = qml[:, :, :, :, :, None, :] * eye[None, None, :, None, None, :, None]
    q_op = q_op.reshape(s, SLAB_STATE, SLAB_K).astype(BF16)

    apl = ap.reshape(t, s, gl, 2, n)
    a_tab = jnp.transpose(apl, (1, 0, 3, 2, 4)).reshape(s, t, SLAB_STATE)
    d_tab = jnp.tile(d_skip.reshape(s, 1, gl * p), (1, 1, t))
    return m_op, p_op, q_op, a_tab, d_tab


def _inproj_a_kernel(x_ref, w_ref, u_ref, c_ref):
    z = jnp.dot(x_ref[...].astype(BF16), w_ref[...], preferred_element_type=F32)
    for k in range(N_SLABS):
        u_ref[k] = z[:, k * LANES:(k + 1) * LANES]
    c_ref[...] = z[:, SSM_WIDTH:]


def _inproj_a(x2, w_a, tm):
    s = x2.shape[0]
    nc = w_a.shape[1] - SSM_WIDTH
    return pl.pallas_call(
        _inproj_a_kernel,
        out_shape=(jax.ShapeDtypeStruct((N_SLABS, s, LANES), F32),
                   jax.ShapeDtypeStruct((s, nc), F32)),
        grid=(s // tm,),
        in_specs=[pl.BlockSpec((tm, D_MODEL), lambda i: (i, 0)),
                  _resident(w_a.shape, lambda i: (0, 0))],
        out_specs=(pl.BlockSpec((N_SLABS, tm, LANES), lambda i: (0, i, 0)),
                   pl.BlockSpec((tm, nc), lambda i: (i, 0))),
        name="inproj_a", compiler_params=_cparams(("parallel",)),
    )(x2, w_a)


def _inproj_g_kernel(x_ref, w_ref, g_ref):
    z = jnp.dot(x_ref[...].astype(BF16), w_ref[...], preferred_element_type=F32)
    g_ref[...] = jax.nn.sigmoid(z).astype(BF16)


def _inproj_g(x2, w_g, tm):
    s = x2.shape[0]
    ng = w_g.shape[1]
    return pl.pallas_call(
        _inproj_g_kernel,
        out_shape=jax.ShapeDtypeStruct((s, ng), BF16),
        grid=(s // tm,),
        in_specs=[pl.BlockSpec((tm, D_MODEL), lambda i: (i, 0)),
                  _resident(w_g.shape, lambda i: (0, 0))],
        out_specs=pl.BlockSpec((tm, ng), lambda i: (i, 0)),
        name="inproj_g", compiler_params=_cparams(("parallel",)),
    )(x2, w_g)


def _cmul(ar, ai, br, bi):
    return ar * br - ai * bi, ar * bi + ai * br


def _ssm_scan_kernel(u_ref, m_ref, p_ref, q_ref, a_ref, d_ref, y_ref, hs_ref, carry_ref, *, rows):
    tb = pl.program_id(1)

    @pl.when(tb == 0)
    def _():
        carry_ref[...] = jnp.zeros_like(carry_ref)

    v32 = jnp.concatenate(
        [u_ref[0, pl.ds(t, rows, stride=SSM_T), :] for t in range(SSM_T)], axis=-1)
    vb = v32.astype(BF16)
    y_intra = jnp.dot(vb, m_ref[0], preferred_element_type=F32)
    x_inj = jnp.dot(vb, p_ref[0], preferred_element_type=F32)

    row = lax.broadcasted_iota(jnp.int32, (rows, HALF_STATE), 0)
    sub = row % SSM_T
    a_tab = a_ref[0]
    xs = pltpu.roll(x_inj, 1, 0)
    first = row == 0
    re = jnp.where(first, carry_ref[:, :HALF_STATE][0:1], xs[:, :HALF_STATE])
    im = jnp.where(first, carry_ref[:, HALF_STATE:][0:1], xs[:, HALF_STATE:])
    for d in (1, 2, 4):
        ar = a_tab[d - 1:d, :HALF_STATE]
        ai = a_tab[d - 1:d, HALF_STATE:]
        keep = sub >= d
        sr = jnp.where(keep, pltpu.roll(re, d, 0), 0.0)
        si = jnp.where(keep, pltpu.roll(im, d, 0), 0.0)
        pr, pi = _cmul(ar, ai, sr, si)
        re = re + pr
        im = im + pi
    hs_ref[:, :HALF_STATE] = re
    hs_ref[:, HALF_STATE:] = im

    tab_r = a_tab[:, :HALF_STATE]
    tab_i = a_tab[:, HALF_STATE:]

    def tile_body(k, last):
        lr, li = last
        r0 = pl.multiple_of(k * SSM_T, SSM_T)
        cr, ci = _cmul(tab_r, tab_i, lr, li)
        hr = hs_ref[pl.ds(r0, SSM_T), :HALF_STATE] + cr
        hi = hs_ref[pl.ds(r0, SSM_T), HALF_STATE:] + ci
        hs_ref[pl.ds(r0, SSM_T), :HALF_STATE] = hr
        hs_ref[pl.ds(r0, SSM_T), HALF_STATE:] = hi
        return (jnp.broadcast_to(hr[SSM_T - 1:SSM_T], (SSM_T, HALF_STATE)),
                jnp.broadcast_to(hi[SSM_T - 1:SSM_T], (SSM_T, HALF_STATE)))

    zero = jnp.zeros((SSM_T, HALF_STATE), F32)
    lr, li = lax.fori_loop(0, rows // SSM_T, tile_body, (zero, zero))

    nr, ni = _cmul(a_tab[0:1, :HALF_STATE], a_tab[0:1, HALF_STATE:], lr, li)
    x_last = x_inj[rows - 1:rows, :]
    carry_ref[:, :HALF_STATE] = nr + x_last[:, :HALF_STATE]
    carry_ref[:, HALF_STATE:] = ni + x_last[:, HALF_STATE:]

    y = (y_intra + jnp.dot(hs_ref[...].astype(BF16), q_ref[0], preferred_element_type=F32)
         + d_ref[0] * v32)
    for t in range(SSM_T):
        y_ref[0, pl.ds(t, rows, stride=SSM_T), :] = y[:, t * LANES:(t + 1) * LANES]


def _ssm_scan(u_slabs, m_op, p_op, q_op, a_tab, d_tab, tb_rows):
    s = u_slabs.shape[1]
    rows = tb_rows // SSM_T
    op_spec = pl.BlockSpec((1, SLAB_K, SLAB_K), lambda i, j: (i, 0, 0))
    return pl.pallas_call(
        functools.partial(_ssm_scan_kernel, rows=rows),
        out_shape=jax.ShapeDtypeStruct((N_SLABS, s, LANES), F32),
        grid=(N_SLABS, s // tb_rows),
        in_specs=[pl.BlockSpec((1, tb_rows, LANES), lambda i, j: (i, j, 0)),
                  op_spec, op_spec, op_spec,
                  pl.BlockSpec((1, SSM_T, SLAB_STATE), lambda i, j: (i, 0, 0)),
                  pl.BlockSpec((1, 1, SLAB_K), lambda i, j: (i, 0, 0))],
        out_specs=pl.BlockSpec((1, tb_rows, LANES), lambda i, j: (i, j, 0)),
        scratch_shapes=[pltpu.VMEM((rows, SLAB_STATE), F32),
                        pltpu.VMEM((SSM_T, SLAB_STATE), F32)],
        name="ssm_scan", compiler_params=_cparams(("parallel", "arbitrary")),
    )(u_slabs, m_op, p_op, q_op, a_tab, d_tab)


def _gelu_tanh(x):
    c = math.sqrt(2.0 / math.pi)
    return 0.5 * x * (1.0 + jnp.tanh(c * (x + 0.044715 * (x * x * x))))


def _glu_kernel(y_ref, w_ref, g_ref, o_ref):
    y = jnp.concatenate([y_ref[k] for k in range(N_SLABS)], axis=-1)
    z = jnp.dot(_gelu_tanh(y).astype(BF16), w_ref[...], preferred_element_type=F32)
    out = z[:, :D_MODEL] * jax.nn.sigmoid(z[:, D_MODEL:]) * g_ref[...].astype(F32)
    o_ref[...] = out.astype(BF16)


def _glu(y_slabs, w_glu, gates, tm):
    s = y_slabs.shape[1]
    return pl.pallas_call(
        _glu_kernel,
        out_shape=jax.ShapeDtypeStruct((s, D_MODEL), BF16),
        grid=(s // tm,),
        in_specs=[pl.BlockSpec((N_SLABS, tm, LANES), lambda i: (0, i, 0)),
                  _resident(w_glu.shape, lambda i: (0, 0)),
                  pl.BlockSpec((tm, D_MODEL), lambda i: (i, 0))],
        out_specs=pl.BlockSpec((tm, D_MODEL), lambda i: (i, 0)),
        name="glu", compiler_params=_cparams(("parallel",)),
    )(y_slabs, w_glu, gates)


def _rms(x, g):
    return x * lax.rsqrt(jnp.mean(x * x, axis=-1, keepdims=True) + RMS_EPS) * g


_NT = (((1,), (1,)), ((), ()))


def _qkv_kernel(cq_ref, ckv_ref, kr_ref, pos_ref, invf_ref, gq_ref, gkv_ref,
                wqt_ref, wk_ref, wvt_ref, qt_ref, k_ref, vt_ref, *, scale):
    tm = cq_ref.shape[0]
    cqn = _rms(cq_ref[...], gq_ref[...]).astype(BF16)
    ckvn = _rms(ckv_ref[...], gkv_ref[...]).astype(BF16)

    ang = invf_ref[...] * pos_ref[...].astype(F32)
    cos_t = jnp.cos(ang)
    sin_t = jnp.sin(ang)

    qt = lax.dot_general(wqt_ref[...], cqn, _NT, preferred_element_type=F32) * scale
    rope_hi = QK_NOPE + QK_ROPE
    for h in range(N_HEADS):
        b = h * QK_PAD
        t1 = qt[b + QK_NOPE:b + QK_NOPE + HALF_ROPE]
        t2 = qt[b + QK_NOPE + HALF_ROPE:b + rope_hi]
        qt_ref[h, 0:QK_NOPE, :] = qt[b:b + QK_NOPE].astype(BF16)
        qt_ref[h, QK_NOPE:QK_NOPE + HALF_ROPE, :] = (t1 * cos_t - t2 * sin_t).astype(BF16)
        qt_ref[h, QK_NOPE + HALF_ROPE:rope_hi, :] = (t1 * sin_t + t2 * cos_t).astype(BF16)
        qt_ref[h, rope_hi:QK_PAD, :] = jnp.zeros((QK_PAD - rope_hi, tm), BF16)

    kr_t = kr_ref[...].T
    k1 = kr_t[0:HALF_ROPE]
    k2 = kr_t[HALF_ROPE:QK_ROPE]
    krot_t = jnp.concatenate(
        [k1 * cos_t - k2 * sin_t, k1 * sin_t + k2 * cos_t,
         jnp.zeros((LANES - QK_ROPE, tm), F32)], axis=0)
    krot = krot_t.T.astype(BF16)

    kn = jnp.dot(ckvn, wk_ref[...], preferred_element_type=F32)
    vt = lax.dot_general(wvt_ref[...], ckvn, _NT, preferred_element_type=F32)
    for h in range(N_HEADS):
        k_ref[h, :, 0:QK_NOPE] = kn[:, h * QK_NOPE:(h + 1) * QK_NOPE].astype(BF16)
        k_ref[h, :, QK_NOPE:QK_PAD] = krot
        vt_ref[h] = vt[h * V_HEAD:(h + 1) * V_HEAD].astype(BF16)


def _qkv(c_lat, pos_row, inv_freq, gq, gkv, wqt, wk, wvt, tm):
    s = c_lat.shape[0]
    scale = 1.0 / math.sqrt(QK_NOPE + QK_ROPE)
    ncq = Q_LORA // Q_LORA
    return pl.pallas_call(
        functools.partial(_qkv_kernel, scale=scale),
        out_shape=(jax.ShapeDtypeStruct((N_HEADS, QK_PAD, s), BF16),
                   jax.ShapeDtypeStruct((N_HEADS, s, QK_PAD), BF16),
                   jax.ShapeDtypeStruct((N_HEADS, V_HEAD, s), BF16)),
        grid=(s // tm,),
        in_specs=[pl.BlockSpec((tm, Q_LORA), lambda i: (i, 0)),
                  pl.BlockSpec((tm, KV_LORA), lambda i: (i, ncq)),
                  pl.BlockSpec((tm, LANES), lambda i: (i, (Q_LORA + KV_LORA) // LANES)),
                  pl.BlockSpec((1, tm), lambda i: (0, i)),
                  _resident((HALF_ROPE, 1), lambda i: (0, 0)),
                  _resident((1, Q_LORA), lambda i: (0, 0)),
                  _resident((1, KV_LORA), lambda i: (0, 0)),
                  _resident(wqt.shape, lambda i: (0, 0)),
                  _resident(wk.shape, lambda i: (0, 0)),
                  _resident(wvt.shape, lambda i: (0, 0))],
        out_specs=(pl.BlockSpec((N_HEADS, QK_PAD, tm), lambda i: (0, 0, i)),
                   pl.BlockSpec((N_HEADS, tm, QK_PAD), lambda i: (0, i, 0)),
                   pl.BlockSpec((N_HEADS, V_HEAD, tm), lambda i: (0, 0, i))),
        name="qkv", compiler_params=_cparams(("parallel",)),
    )(c_lat, c_lat, c_lat, pos_row, inv_freq, gq, gkv, wqt, wk, wvt)


_NEG = -1e30


def _attn_kernel(qt_ref, k_ref, vt_ref, o_ref, *, tq, tk):
    i = pl.program_id(1)
    q = qt_ref[0]

    def step(ks, carry, mask):
        m, l, acc = carry
        kt = k_ref[0, pl.ds(ks, tk), :]
        s = jnp.dot(kt, q, preferred_element_type=F32)
        if mask is not None:
            s = jnp.where(mask, s, _NEG)
        m_new = jnp.maximum(m, jnp.max(s, axis=0, keepdims=True))
        p = jnp.exp(s - m_new)
        alpha = jnp.exp(m - m_new)
        l = alpha * l + jnp.sum(p, axis=0, keepdims=True)
        vt = vt_ref[0, :, pl.ds(ks, tk)]
        acc = alpha * acc + jnp.dot(vt, p.astype(BF16), preferred_element_type=F32)
        return m_new, l, acc

    init = (jnp.full((1, tq), _NEG, F32), jnp.zeros((1, tq), F32), jnp.zeros((V_HEAD, tq), F32))
    n_full = (i * tq) // tk
    carry = lax.fori_loop(
        0, n_full, lambda j, c: step(pl.multiple_of(j * tk, tk), c, None), init)
    krow = lax.broadcasted_iota(jnp.int32, (tk, tq), 0)
    qcol = lax.broadcasted_iota(jnp.int32, (tk, tq), 1)
    for d in range(tq // tk):
        ks = pl.multiple_of(i * tq + d * tk, tk)
        carry = step(ks, carry, krow + d * tk <= qcol)
    m, l, acc = carry
    o_ref[...] = (acc * (1.0 / l)).T.astype(BF16)


def _attention(qt, k, vt, tq, tk):
    s = k.shape[1]
    return pl.pallas_call(
        functools.partial(_attn_kernel, tq=tq, tk=tk),
        out_shape=jax.ShapeDtypeStruct((s, N_HEADS * V_HEAD), BF16),
        grid=(N_HEADS, s // tq),
        in_specs=[pl.BlockSpec((1, QK_PAD, tq), lambda h, i: (h, 0, i)),
                  pl.BlockSpec((1, s, QK_PAD), lambda h, i: (h, 0, 0)),
                  pl.BlockSpec((1, V_HEAD, s), lambda h, i: (h, 0, 0))],
        out_specs=pl.BlockSpec((tq, V_HEAD), lambda h, i: (i, h)),
        name="attn", compiler_params=_cparams(("parallel", "parallel")),
    )(qt, k, vt)


def _layer_norm(r, g, b):
    mu = jnp.mean(r, axis=-1, keepdims=True)
    c = r - mu
    var = jnp.mean(c * c, axis=-1, keepdims=True)
    return c * lax.rsqrt(var + LN_EPS) * g + b


def _outproj_kernel(sg_ref, gm_ref, mla_ref, x_ref, w_ref, g_ref, b_ref, h_ref):
    merged = sg_ref[...].astype(F32) + gm_ref[...].astype(F32) * mla_ref[...].astype(F32)
    mix = jnp.dot(merged.astype(BF16), w_ref[...], preferred_element_type=F32)
    h_ref[...] = _layer_norm(DEEPNORM_ALPHA * x_ref[...] + mix, g_ref[...], b_ref[...])


def _outproj(ssm_gated, gates, mla, x2, w_out, ln_g, ln_b, tm):
    s = x2.shape[0]
    row = lambda i: (i, 0)
    return pl.pallas_call(
        _outproj_kernel,
        out_shape=jax.ShapeDtypeStruct((s, D_MODEL), F32),
        grid=(s // tm,),
        in_specs=[pl.BlockSpec((tm, D_MODEL), row),
                  pl.BlockSpec((tm, D_MODEL), lambda i: (i, 1)),
                  pl.BlockSpec((tm, D_MODEL), row),
                  pl.BlockSpec((tm, D_MODEL), row),
                  _resident(w_out.shape, lambda i: (0, 0)),
                  _resident((1, D_MODEL), lambda i: (0, 0)),
                  _resident((1, D_MODEL), lambda i: (0, 0))],
        out_specs=pl.BlockSpec((tm, D_MODEL), row),
        name="outproj", compiler_params=_cparams(("parallel",)),
    )(ssm_gated, gates, mla, x2, w_out, ln_g, ln_b)


def _ffn_kernel(h_ref, wg_ref, wu_ref, wd_ref, g_ref, b_ref, o_ref, hb_ref, acc_ref):
    j = pl.program_id(1)

    @pl.when(j == 0)
    def _():
        hb_ref[...] = h_ref[...].astype(BF16)
        acc_ref[...] = jnp.zeros_like(acc_ref)

    hb = hb_ref[...]
    gate = jnp.dot(hb, wg_ref[...], preferred_element_type=F32)
    up = jnp.dot(hb, wu_ref[...], preferred_element_type=F32)
    act = (gate * jax.nn.sigmoid(gate) * up).astype(BF16)
    acc_ref[...] += jnp.dot(act, wd_ref[...], preferred_element_type=F32)

    @pl.when(j == pl.num_programs(1) - 1)
    def _():
        o_ref[...] = _layer_norm(DEEPNORM_ALPHA * h_ref[...] + acc_ref[...], g_ref[...], b_ref[...])


def _ffn(h1, wg, wu, wd, ln_g, ln_b, tm, tf):
    s = h1.shape[0]
    dff = wg.shape[1]
    return pl.pallas_call(
        _ffn_kernel,
        out_shape=jax.ShapeDtypeStruct((s, D_MODEL), F32),
        grid=(s // tm, dff // tf),
        in_specs=[pl.BlockSpec((tm, D_MODEL), lambda i, j: (i, 0)),
                  pl.BlockSpec((D_MODEL, tf), lambda i, j: (0, j)),
                  pl.BlockSpec((D_MODEL, tf), lambda i, j: (0, j)),
                  pl.BlockSpec((tf, D_MODEL), lambda i, j: (j, 0)),
                  _resident((1, D_MODEL), lambda i, j: (0, 0)),
                  _resident((1, D_MODEL), lambda i, j: (0, 0))],
        out_specs=pl.BlockSpec((tm, D_MODEL), lambda i, j: (i, 0)),
        scratch_shapes=[pltpu.VMEM((tm, D_MODEL), BF16), pltpu.VMEM((tm, D_MODEL), F32)],
        name="ffn", compiler_params=_cparams(("parallel", "arbitrary")),
    )(h1, wg, wu, wd, ln_g, ln_b)


def _tile(s, want):
    t = min(s, want)
    assert s % t == 0, (s, t)
    return t


def kernel(x, positions, w_in, ssm_lambda_re, ssm_lambda_im, ssm_log_dt, ssm_b_re, ssm_b_im,
           ssm_c_re, ssm_c_im, ssm_d, w_glu, q_norm_g, w_uq, kv_norm_g, w_ukv, w_out,
           ln1_g, ln1_b, w_ffn_gate, w_ffn_up, w_ffn_down, ln2_g, ln2_b):
    bsz, seq, d_model = x.shape
    assert bsz == 1 and d_model == D_MODEL and w_in.shape[0] == DEPTH
    x2 = x.reshape(seq, D_MODEL)
    pos_row = positions.reshape(1, seq)
    inv_freq = (1.0 / (ROPE_THETA ** (jnp.arange(0, QK_ROPE, 2, dtype=F32) / QK_ROPE))
                ).reshape(HALF_ROPE, 1)
    h = x2
    for l in range(DEPTH):
        lat_hi = SSM_WIDTH + Q_LORA + KV_LORA + QK_ROPE
        w_a = jnp.pad(w_in[l][:, :lat_hi], ((0, 0), (0, LANES - QK_ROPE))).astype(BF16)
        w_g = w_in[l][:, lat_hi:].astype(BF16)
        wq = jnp.pad(w_uq[l].reshape(Q_LORA, N_HEADS, QK_NOPE + QK_ROPE),
                     ((0, 0), (0, 0), (0, QK_PAD - QK_NOPE - QK_ROPE)))
        wqt = wq.reshape(Q_LORA, N_HEADS * QK_PAD).T.astype(BF16)
        wkv = w_ukv[l].reshape(KV_LORA, N_HEADS, QK_NOPE + V_HEAD)
        wk = wkv[:, :, :QK_NOPE].reshape(KV_LORA, N_HEADS * QK_NOPE).astype(BF16)
        wvt = wkv[:, :, QK_NOPE:].reshape(KV_LORA, N_HEADS * V_HEAD).T.astype(BF16)

        u_slabs, c_lat = _inproj_a(h, w_a, _tile(seq, 512))
        gates = _inproj_g(h, w_g, _tile(seq, 512))

        kt, pm, qm, ap = _ssm_prep(ssm_lambda_re[l], ssm_lambda_im[l], ssm_log_dt[l],
                                   ssm_b_re[l], ssm_b_im[l], ssm_c_re[l], ssm_c_im[l])
        m_op, p_op, q_op, a_tab, d_tab = _ssm_operators(kt, pm, qm, ap, ssm_d[l])
        y_slabs = _ssm_scan(u_slabs, m_op, p_op, q_op, a_tab, d_tab, _tile(seq, 2048))
        ssm_gated = _glu(y_slabs, w_glu[l].astype(BF16), gates, _tile(seq, 256))

        qt, k, vt = _qkv(c_lat, pos_row, inv_freq, q_norm_g[l].reshape(1, Q_LORA),
                         kv_norm_g[l].reshape(1, KV_LORA), wqt, wk, wvt, _tile(seq, 256))
        mla = _attention(qt, k, vt, _tile(seq, 512), _tile(seq, 512))

        h = _outproj(ssm_gated, gates, mla, h, w_out[l].astype(BF16),
                     ln1_g[l].reshape(1, D_MODEL), ln1_b[l].reshape(1, D_MODEL), _tile(seq, 256))

        h = _ffn(h, w_ffn_gate[l].astype(BF16), w_ffn_up[l].astype(BF16),
                 w_ffn_down[l].astype(BF16), ln2_g[l].reshape(1, D_MODEL),
                 ln2_b[l].reshape(1, D_MODEL), _tile(seq, 512), 512)
    return h.reshape(bsz, seq, D_MODEL)
```

```python
import functools
import math

import jax
import jax.numpy as jnp
from jax import lax
from jax.experimental import pallas as pl
from jax.experimental.pallas import tpu as pltpu

F32 = jnp.float32
BF16 = jnp.bfloat16

D_MODEL = 2048
SSM_GROUP = 16
SSM_WIDTH = D_MODEL // 2
SSM_GROUPS = SSM_WIDTH // SSM_GROUP
SSM_STATE = 64
N_HEADS = 16
QK_NOPE = 128
QK_ROPE = 64
V_HEAD = 128
Q_LORA = 512
KV_LORA = 512
ROPE_THETA = 10000.0
DEPTH = 1
DEEPNORM_ALPHA = (2.0 * DEPTH) ** 0.25
LN_EPS = 1e-5
RMS_EPS = 1e-6

LANES = 128
V7X_VMEM_BYTES = 64 * 1024 * 1024
VMEM_LIMIT = 56 * 1024 * 1024

HALF_ROPE = QK_ROPE // 2
QK_PAD = 256
SSM_T = 8
SLAB_GROUPS = LANES // SSM_GROUP
N_SLABS = SSM_WIDTH // LANES
SLAB_K = SSM_T * LANES
SLAB_STATE = SLAB_GROUPS * 2 * SSM_STATE
HALF_STATE = SLAB_STATE // 2
V_ROWS = V_HEAD + 16

ROWS_INPROJ = 512
ROWS_SSM = 2048
ROWS_GLU = 512
ROWS_QKV = 512
ROWS_ATTN = 512
ROWS_OUTPROJ = 512
ROWS_FFN = 512
COLS_FFN = 512


def _cparams(sem, vmem=VMEM_LIMIT):
    return pltpu.CompilerParams(dimension_semantics=sem, vmem_limit_bytes=vmem)


def _resident(shape, index_map):
    return pl.BlockSpec(shape, index_map, pipeline_mode=pl.Buffered(1))


def _ssm_prep_kernel(lre_ref, lim_ref, ldt_ref, btr_ref, bti_ref, cr_ref, ci_ref,
                     lre_flat_ref, lim_flat_ref, ldt_flat_ref, m_ref, p_ref, q_ref, a_ref):
    gl, n, pp, t_len = SLAB_GROUPS, SSM_STATE, SSM_GROUP, SSM_T
    lre = lre_ref[...]
    lim = lim_ref[...]
    dt = jnp.exp(ldt_ref[...])

    def apow(xr, xi, k):
        mag = jnp.exp(xr * float(k))
        ang = xi * float(k)
        return mag * jnp.cos(ang), mag * jnp.sin(ang)

    xr = lre * dt
    xi = lim * dt
    ar, ai = apow(xr, xi, 1)
    den = lre * lre + lim * lim
    nr = ar - 1.0
    coef_re = (nr * lre + ai * lim) / den
    coef_im = (ai * lre - nr * lim) / den
    btr = btr_ref[...]
    bti = bti_ref[...]
    bbr = coef_re * btr - coef_im * bti
    bbi = coef_re * bti + coef_im * btr
    cr = cr_ref[...]
    ci = ci_ref[...]

    def spread(width, period):
        r = lax.broadcasted_iota(jnp.int32, (period, width), 0)
        c = lax.broadcasted_iota(jnp.int32, (period, width), 1)
        return (c % period == r).astype(BF16)

    def same_group(rows, row_period, cols, col_period):
        r = lax.broadcasted_iota(jnp.int32, (rows, cols), 0)
        c = lax.broadcasted_iota(jnp.int32, (rows, cols), 1)
        return r // row_period == c // col_period

    rep_p = spread(LANES, pp)
    rep_n = spread(HALF_STATE, n)
    mask_pp = same_group(LANES, pp, LANES, pp)
    mask_pn = same_group(LANES, pp, HALF_STATE, n)

    def block_diag(x, rep, mask):
        x2 = x.reshape(gl * pp, x.shape[-1]).astype(BF16)
        return jnp.where(mask, jnp.dot(x2, rep, preferred_element_type=F32), 0.0)

    m_ref[...] = jnp.zeros_like(m_ref)
    dn = (((2,), (2,)), ((0,), (0,)))
    for k in range(t_len + 1):
        pr, pi = (jnp.ones_like(xr), jnp.zeros_like(xr)) if k == 0 else apow(xr, xi, k)
        car = cr * pr - ci * pi
        cai = cr * pi + ci * pr
        if k < t_len:
            resp = (lax.dot_general(bbr, car, dn, precision=lax.Precision.HIGHEST,
                                    preferred_element_type=F32)
                    - lax.dot_general(bbi, cai, dn, precision=lax.Precision.HIGHEST,
                                      preferred_element_type=F32))
            tile = block_diag(resp, rep_p, mask_pp).astype(BF16)
            for t0 in range(t_len - k):
                m_ref[0, t0 * LANES:(t0 + 1) * LANES, (t0 + k) * LANES:(t0 + k + 1) * LANES] = tile
            t0 = t_len - 1 - k
            for ri, inj in enumerate((pr * bbr - pi * bbi, pr * bbi + pi * bbr)):
                p_ref[0, t0 * LANES:(t0 + 1) * LANES, ri * HALF_STATE:(ri + 1) * HALF_STATE] = (
                    block_diag(inj, rep_n, mask_pn).astype(BF16))
        if k >= 1:
            for ri, ca in enumerate((car, -cai)):
                z = block_diag(ca, rep_n, mask_pn)
                q_ref[0, ri * HALF_STATE:(ri + 1) * HALF_STATE, (k - 1) * LANES:k * LANES] = (
                    z.T.astype(BF16))

    dtf = jnp.exp(ldt_flat_ref[0])
    xrf = lre_flat_ref[0] * dtf
    xif = lim_flat_ref[0] * dtf
    for j in range(1, t_len + 1):
        pr, pi = apow(xrf, xif, t_len * j)
        a_ref[0, j - 1:j, 0:HALF_STATE] = pr
        a_ref[0, j - 1:j, HALF_STATE:] = pi


def _ssm_prep(lam_re, lam_im, log_dt, b_re, b_im, c_re, c_im):
    g, n, p, s = SSM_GROUPS, SSM_STATE, SSM_GROUP, N_SLABS
    gl = SLAB_GROUPS
    grp = lambda shape: pl.BlockSpec((gl,) + shape, lambda i: (i, 0, 0))
    flat = pl.BlockSpec((1, 1, HALF_STATE), lambda i: (i, 0, 0))
    op = pl.BlockSpec((1, SLAB_K, SLAB_K), lambda i: (i, 0, 0))
    return pl.pallas_call(
        _ssm_prep_kernel,
        out_shape=(jax.ShapeDtypeStruct((s, SLAB_K, SLAB_K), BF16),
                   jax.ShapeDtypeStruct((s, SLAB_K, SLAB_STATE), BF16),
                   jax.ShapeDtypeStruct((s, SLAB_STATE, SLAB_K), BF16),
                   jax.ShapeDtypeStruct((s, SSM_T, SLAB_STATE), F32)),
        grid=(s,),
        in_specs=[grp((1, n)), grp((1, n)), grp((1, 1)), grp((p, n)), grp((p, n)),
                  grp((p, n)), grp((p, n)), flat, flat, flat],
        out_specs=(op, op, op, pl.BlockSpec((1, SSM_T, SLAB_STATE), lambda i: (i, 0, 0))),
        name="ssm_prep", compiler_params=_cparams(("parallel",)),
    )(lam_re.reshape(g, 1, n), lam_im.reshape(g, 1, n), log_dt.reshape(g, 1, 1),
      jnp.swapaxes(b_re, 1, 2), jnp.swapaxes(b_im, 1, 2), c_re, c_im,
      lam_re.reshape(s, 1, HALF_STATE), lam_im.reshape(s, 1, HALF_STATE),
      jnp.repeat(log_dt, n).reshape(s, 1, HALF_STATE))


def _inproj_a_kernel(x_ref, w_ref, u_ref, c_ref):
    z = jnp.dot(x_ref[...].astype(BF16), w_ref[...], preferred_element_type=F32)
    for k in range(N_SLABS):
        u_ref[k] = z[:, k * LANES:(k + 1) * LANES]
    c_ref[...] = z[:, SSM_WIDTH:]


def _inproj_a(x2, w_a, tm):
    s = x2.shape[0]
    nc = w_a.shape[1] - SSM_WIDTH
    return pl.pallas_call(
        _inproj_a_kernel,
        out_shape=(jax.ShapeDtypeStruct((N_SLABS, s, LANES), F32),
                   jax.ShapeDtypeStruct((s, nc), F32)),
        grid=(s // tm,),
        in_specs=[pl.BlockSpec((tm, D_MODEL), lambda i: (i, 0)),
                  _resident(w_a.shape, lambda i: (0, 0))],
        out_specs=(pl.BlockSpec((N_SLABS, tm, LANES), lambda i: (0, i, 0)),
                   pl.BlockSpec((tm, nc), lambda i: (i, 0))),
        name="inproj_a", compiler_params=_cparams(("parallel",)),
    )(x2, w_a)


def _inproj_g_kernel(x_ref, w_ref, g_ref):
    z = jnp.dot(x_ref[...].astype(BF16), w_ref[...], preferred_element_type=F32)
    g_ref[...] = jax.nn.sigmoid(z).astype(BF16)


def _inproj_g(x2, w_g, tm):
    s = x2.shape[0]
    ng = w_g.shape[1]
    return pl.pallas_call(
        _inproj_g_kernel,
        out_shape=jax.ShapeDtypeStruct((s, ng), BF16),
        grid=(s // tm,),
        in_specs=[pl.BlockSpec((tm, D_MODEL), lambda i: (i, 0)),
                  _resident(w_g.shape, lambda i: (0, 0))],
        out_specs=pl.BlockSpec((tm, ng), lambda i: (i, 0)),
        name="inproj_g", compiler_params=_cparams(("parallel",)),
    )(x2, w_g)


def _cmul(ar, ai, br, bi):
    return ar * br - ai * bi, ar * bi + ai * br


def _ssm_scan_kernel(u_ref, m_ref, p_ref, q_ref, a_ref, d_ref, y_ref, hs_ref, carry_ref, *, rows):
    tb = pl.program_id(1)

    @pl.when(tb == 0)
    def _():
        carry_ref[...] = jnp.zeros_like(carry_ref)

    v32 = jnp.concatenate(
        [u_ref[0, pl.ds(t, rows, stride=SSM_T), :] for t in range(SSM_T)], axis=-1)
    vb = v32.astype(BF16)
    y_intra = jnp.dot(vb, m_ref[0], preferred_element_type=F32)
    x_inj = jnp.dot(vb, p_ref[0], preferred_element_type=F32)

    row = lax.broadcasted_iota(jnp.int32, (rows, HALF_STATE), 0)
    sub = row % SSM_T
    a_tab = a_ref[0]
    xs = pltpu.roll(x_inj, 1, 0)
    first = row == 0
    re = jnp.where(first, carry_ref[:, :HALF_STATE][0:1], xs[:, :HALF_STATE])
    im = jnp.where(first, carry_ref[:, HALF_STATE:][0:1], xs[:, HALF_STATE:])
    for d in (1, 2, 4):
        ar = a_tab[d - 1:d, :HALF_STATE]
        ai = a_tab[d - 1:d, HALF_STATE:]
        keep = sub >= d
        sr = jnp.where(keep, pltpu.roll(re, d, 0), 0.0)
        si = jnp.where(keep, pltpu.roll(im, d, 0), 0.0)
        pr, pi = _cmul(ar, ai, sr, si)
        re = re + pr
        im = im + pi
    hs_ref[:, :HALF_STATE] = re
    hs_ref[:, HALF_STATE:] = im

    tab_r = a_tab[:, :HALF_STATE]
    tab_i = a_tab[:, HALF_STATE:]

    def tile_body(k, last):
        lr, li = last
        r0 = pl.multiple_of(k * SSM_T, SSM_T)
        cr, ci = _cmul(tab_r, tab_i, lr, li)
        hr = hs_ref[pl.ds(r0, SSM_T), :HALF_STATE] + cr
        hi = hs_ref[pl.ds(r0, SSM_T), HALF_STATE:] + ci
        hs_ref[pl.ds(r0, SSM_T), :HALF_STATE] = hr
        hs_ref[pl.ds(r0, SSM_T), HALF_STATE:] = hi
        return (jnp.broadcast_to(hr[SSM_T - 1:SSM_T], (SSM_T, HALF_STATE)),
                jnp.broadcast_to(hi[SSM_T - 1:SSM_T], (SSM_T, HALF_STATE)))

    zero = jnp.zeros((SSM_T, HALF_STATE), F32)
    lr, li = lax.fori_loop(0, rows // SSM_T, tile_body, (zero, zero))

    nr, ni = _cmul(a_tab[0:1, :HALF_STATE], a_tab[0:1, HALF_STATE:], lr, li)
    x_last = x_inj[rows - 1:rows, :]
    carry_ref[:, :HALF_STATE] = nr + x_last[:, :HALF_STATE]
    carry_ref[:, HALF_STATE:] = ni + x_last[:, HALF_STATE:]

    y = (y_intra + jnp.dot(hs_ref[...].astype(BF16), q_ref[0], preferred_element_type=F32)
         + d_ref[0] * v32)
    for t in range(SSM_T):
        y_ref[0, pl.ds(t, rows, stride=SSM_T), :] = y[:, t * LANES:(t + 1) * LANES]


def _ssm_scan(u_slabs, m_op, p_op, q_op, a_tab, d_tab, tb_rows):
    s = u_slabs.shape[1]
    rows = tb_rows // SSM_T
    op_spec = pl.BlockSpec((1, SLAB_K, SLAB_K), lambda i, j: (i, 0, 0))
    return pl.pallas_call(
        functools.partial(_ssm_scan_kernel, rows=rows),
        out_shape=jax.ShapeDtypeStruct((N_SLABS, s, LANES), F32),
        grid=(N_SLABS, s // tb_rows),
        in_specs=[pl.BlockSpec((1, tb_rows, LANES), lambda i, j: (i, j, 0)),
                  op_spec, op_spec, op_spec,
                  pl.BlockSpec((1, SSM_T, SLAB_STATE), lambda i, j: (i, 0, 0)),
                  pl.BlockSpec((1, 1, SLAB_K), lambda i, j: (i, 0, 0))],
        out_specs=pl.BlockSpec((1, tb_rows, LANES), lambda i, j: (i, j, 0)),
        scratch_shapes=[pltpu.VMEM((rows, SLAB_STATE), F32),
                        pltpu.VMEM((SSM_T, SLAB_STATE), F32)],
        name="ssm_scan", compiler_params=_cparams(("parallel", "arbitrary")),
    )(u_slabs, m_op, p_op, q_op, a_tab, d_tab)


def _gelu_tanh(x):
    c = math.sqrt(2.0 / math.pi)
    return 0.5 * x * (1.0 + jnp.tanh(c * (x + 0.044715 * (x * x * x))))


def _glu_kernel(y_ref, w_ref, g_ref, o_ref):
    y = jnp.concatenate([y_ref[k] for k in range(N_SLABS)], axis=-1)
    z = jnp.dot(_gelu_tanh(y).astype(BF16), w_ref[...], preferred_element_type=F32)
    out = z[:, :D_MODEL] * jax.nn.sigmoid(z[:, D_MODEL:]) * g_ref[...].astype(F32)
    o_ref[...] = out.astype(BF16)


def _glu(y_slabs, w_glu, gates, tm):
    s = y_slabs.shape[1]
    return pl.pallas_call(
        _glu_kernel,
        out_shape=jax.ShapeDtypeStruct((s, D_MODEL), BF16),
        grid=(s // tm,),
        in_specs=[pl.BlockSpec((N_SLABS, tm, LANES), lambda i: (0, i, 0)),
                  _resident(w_glu.shape, lambda i: (0, 0)),
                  pl.BlockSpec((tm, D_MODEL), lambda i: (i, 0))],
        out_specs=pl.BlockSpec((tm, D_MODEL), lambda i: (i, 0)),
        name="glu", compiler_params=_cparams(("parallel",)),
    )(y_slabs, w_glu, gates)


def _rms(x, g):
    return x * lax.rsqrt(jnp.mean(x * x, axis=-1, keepdims=True) + RMS_EPS) * g


_NT = (((1,), (1,)), ((), ()))


def _qkv_kernel(cq_ref, ckv_ref, kr_ref, pos_ref, invf_ref, gq_ref, gkv_ref,
                wqt_ref, wk_ref, wvt_ref, qt_ref, k_ref, vt_ref, *, scale):
    tm = cq_ref.shape[0]
    cqn = _rms(cq_ref[...], gq_ref[...]).astype(BF16)
    ckvn = _rms(ckv_ref[...], gkv_ref[...]).astype(BF16)

    ang = invf_ref[...] * pos_ref[...].astype(F32)
    cos_t = jnp.cos(ang)
    sin_t = jnp.sin(ang)

    qt = lax.dot_general(wqt_ref[...], cqn, _NT, preferred_element_type=F32) * scale
    rope_hi = QK_NOPE + QK_ROPE
    for h in range(N_HEADS):
        b = h * QK_PAD
        t1 = qt[b + QK_NOPE:b + QK_NOPE + HALF_ROPE]
        t2 = qt[b + QK_NOPE + HALF_ROPE:b + rope_hi]
        qt_ref[h, 0:QK_NOPE, :] = qt[b:b + QK_NOPE].astype(BF16)
        qt_ref[h, QK_NOPE:QK_NOPE + HALF_ROPE, :] = (t1 * cos_t - t2 * sin_t).astype(BF16)
        qt_ref[h, QK_NOPE + HALF_ROPE:rope_hi, :] = (t1 * sin_t + t2 * cos_t).astype(BF16)
        qt_ref[h, rope_hi:QK_PAD, :] = jnp.zeros((QK_PAD - rope_hi, tm), BF16)

    kr_t = kr_ref[...].T
    k1 = kr_t[0:HALF_ROPE]
    k2 = kr_t[HALF_ROPE:QK_ROPE]
    krot_t = jnp.concatenate(
        [k1 * cos_t - k2 * sin_t, k1 * sin_t + k2 * cos_t,
         jnp.zeros((LANES - QK_ROPE, tm), F32)], axis=0)
    krot = krot_t.T.astype(BF16)

    kn = jnp.dot(ckvn, wk_ref[...], preferred_element_type=F32)
    vt = lax.dot_general(wvt_ref[...], ckvn, _NT, preferred_element_type=F32)
    for h in range(N_HEADS):
        k_ref[h, :, 0:QK_NOPE] = kn[:, h * QK_NOPE:(h + 1) * QK_NOPE].astype(BF16)
        k_ref[h, :, QK_NOPE:QK_PAD] = krot
        vt_ref[h, 0:V_HEAD, :] = vt[h * V_HEAD:(h + 1) * V_HEAD].astype(BF16)
        vt_ref[h, V_HEAD:V_ROWS, :] = jnp.ones((V_ROWS - V_HEAD, tm), BF16)


def _qkv(c_lat, pos_row, inv_freq, gq, gkv, wqt, wk, wvt, tm):
    s = c_lat.shape[0]
    scale = math.log2(math.e) / math.sqrt(QK_NOPE + QK_ROPE)
    ncq = Q_LORA // Q_LORA
    return pl.pallas_call(
        functools.partial(_qkv_kernel, scale=scale),
        out_shape=(jax.ShapeDtypeStruct((N_HEADS, QK_PAD, s), BF16),
                   jax.ShapeDtypeStruct((N_HEADS, s, QK_PAD), BF16),
                   jax.ShapeDtypeStruct((N_HEADS, V_ROWS, s), BF16)),
        grid=(s // tm,),
        in_specs=[pl.BlockSpec((tm, Q_LORA), lambda i: (i, 0)),
                  pl.BlockSpec((tm, KV_LORA), lambda i: (i, ncq)),
                  pl.BlockSpec((tm, LANES), lambda i: (i, (Q_LORA + KV_LORA) // LANES)),
                  pl.BlockSpec((1, tm), lambda i: (0, i)),
                  _resident((HALF_ROPE, 1), lambda i: (0, 0)),
                  _resident((1, Q_LORA), lambda i: (0, 0)),
                  _resident((1, KV_LORA), lambda i: (0, 0)),
                  _resident(wqt.shape, lambda i: (0, 0)),
                  _resident(wk.shape, lambda i: (0, 0)),
                  _resident(wvt.shape, lambda i: (0, 0))],
        out_specs=(pl.BlockSpec((N_HEADS, QK_PAD, tm), lambda i: (0, 0, i)),
                   pl.BlockSpec((N_HEADS, tm, QK_PAD), lambda i: (0, i, 0)),
                   pl.BlockSpec((N_HEADS, V_ROWS, tm), lambda i: (0, 0, i))),
        name="qkv", compiler_params=_cparams(("parallel",)),
    )(c_lat, c_lat, c_lat, pos_row, inv_freq, gq, gkv, wqt, wk, wvt)


_NEG = -1e30


ATTN_UNROLL = 8


def _attn_kernel(qt_ref, k_ref, vt_ref, o_ref, s_ref, mx_ref, acc_ref, m_ref, *, t):
    i = pl.program_id(1)
    q = qt_ref[0]
    n_full = i
    first = lax.rem(n_full, 2)
    kd = pl.multiple_of(i * t, t)

    def qk(ks):
        return jnp.dot(k_ref[0, pl.ds(ks, t), :], q, preferred_element_type=F32)

    def pv(ks, p):
        return jnp.dot(vt_ref[0, :, pl.ds(ks, t)], p, preferred_element_type=F32)

    def put_scores(slot, ks):
        s = qk(ks)
        s_ref[slot] = s
        mx_ref[slot] = jnp.max(s, axis=0, keepdims=True)

    put_scores(first, 0)
    krow = lax.broadcasted_iota(jnp.int32, (t, t), 0)
    qcol = lax.broadcasted_iota(jnp.int32, (t, t), 1)
    sd = jnp.where(krow <= qcol, qk(kd), _NEG)
    m0 = jnp.max(sd, axis=0, keepdims=True)
    m_ref[...] = m0
    acc_ref[...] = pv(kd, jnp.exp2(sd - m0).astype(BF16))

    def item(slot, j):
        put_scores(1 - slot, pl.multiple_of(jnp.minimum(j + 1, n_full - 1) * t, t))
        m_old = m_ref[...]
        m_new = jnp.maximum(m_old, mx_ref[slot])
        p = jnp.exp2(s_ref[slot] - m_new).astype(BF16)
        acc_ref[...] = jnp.exp2(m_old - m_new) * acc_ref[...] + pv(pl.multiple_of(j * t, t), p)
        m_ref[...] = m_new

    def run(width, j0):
        for u in range(width):
            item(u % 2, j0 + u)

    @pl.when(first == 1)
    def _():
        item(1, 0)

    done = first
    width = 2
    while width < ATTN_UNROLL:
        bit = lax.rem(n_full // width, 2)
        pl.when(bit == 1)(functools.partial(run, width, done))
        done = done + bit * width
        width *= 2

    def main(r, carry):
        run(ATTN_UNROLL, done + ATTN_UNROLL * r)
        return carry

    lax.fori_loop(0, n_full // ATTN_UNROLL, main, 0)

    acc = acc_ref[...]
    o_ref[...] = (acc[:V_HEAD] * (1.0 / acc[V_HEAD:V_HEAD + 1])).T.astype(BF16)


def _attention(qt, k, vt, t):
    s = k.shape[1]
    return pl.pallas_call(
        functools.partial(_attn_kernel, t=t),
        out_shape=jax.ShapeDtypeStruct((s, N_HEADS * V_HEAD), BF16),
        grid=(N_HEADS, s // t),
        in_specs=[pl.BlockSpec((1, QK_PAD, t), lambda h, i: (h, 0, i)),
                  pl.BlockSpec((1, s, QK_PAD), lambda h, i: (h, 0, 0)),
                  pl.BlockSpec((1, V_ROWS, s), lambda h, i: (h, 0, 0))],
        out_specs=pl.BlockSpec((t, V_HEAD), lambda h, i: (i, h)),
        scratch_shapes=[pltpu.VMEM((2, t, t), F32), pltpu.VMEM((2, 1, t), F32),
                        pltpu.VMEM((V_ROWS, t), F32), pltpu.VMEM((1, t), F32)],
        name="attn", compiler_params=_cparams(("parallel", "arbitrary")),
    )(qt, k, vt)


def _layer_norm(r, g, b):
    mu = jnp.mean(r, axis=-1, keepdims=True)
    c = r - mu
    var = jnp.mean(c * c, axis=-1, keepdims=True)
    return c * lax.rsqrt(var + LN_EPS) * g + b


def _outproj_kernel(sg_ref, gm_ref, mla_ref, x_ref, w_ref, g_ref, b_ref, h_ref):
    merged = sg_ref[...].astype(F32) + gm_ref[...].astype(F32) * mla_ref[...].astype(F32)
    mix = jnp.dot(merged.astype(BF16), w_ref[...], preferred_element_type=F32)
    h_ref[...] = _layer_norm(DEEPNORM_ALPHA * x_ref[...] + mix, g_ref[...], b_ref[...])


def _outproj(ssm_gated, gates, mla, x2, w_out, ln_g, ln_b, tm):
    s = x2.shape[0]
    row = lambda i: (i, 0)
    return pl.pallas_call(
        _outproj_kernel,
        out_shape=jax.ShapeDtypeStruct((s, D_MODEL), F32),
        grid=(s // tm,),
        in_specs=[pl.BlockSpec((tm, D_MODEL), row),
                  pl.BlockSpec((tm, D_MODEL), lambda i: (i, 1)),
                  pl.BlockSpec((tm, D_MODEL), row),
                  pl.BlockSpec((tm, D_MODEL), row),
                  _resident(w_out.shape, lambda i: (0, 0)),
                  _resident((1, D_MODEL), lambda i: (0, 0)),
                  _resident((1, D_MODEL), lambda i: (0, 0))],
        out_specs=pl.BlockSpec((tm, D_MODEL), row),
        name="outproj", compiler_params=_cparams(("parallel",)),
    )(ssm_gated, gates, mla, x2, w_out, ln_g, ln_b)


def _ffn_kernel(h_ref, wg_ref, wu_ref, wd_ref, g_ref, b_ref, o_ref, hb_ref, acc_ref):
    j = pl.program_id(1)

    @pl.when(j == 0)
    def _():
        hb_ref[...] = h_ref[...].astype(BF16)
        acc_ref[...] = jnp.zeros_like(acc_ref)

    hb = hb_ref[...]
    gate = jnp.dot(hb, wg_ref[...], preferred_element_type=F32)
    up = jnp.dot(hb, wu_ref[...], preferred_element_type=F32)
    act = (gate * jax.nn.sigmoid(gate) * up).astype(BF16)
    acc_ref[...] += jnp.dot(act, wd_ref[...], preferred_element_type=F32)

    @pl.when(j == pl.num_programs(1) - 1)
    def _():
        o_ref[...] = _layer_norm(DEEPNORM_ALPHA * h_ref[...] + acc_ref[...], g_ref[...], b_ref[...])


def _ffn(h1, wg, wu, wd, ln_g, ln_b, tm, tf):
    s = h1.shape[0]
    dff = wg.shape[1]
    return pl.pallas_call(
        _ffn_kernel,
        out_shape=jax.ShapeDtypeStruct((s, D_MODEL), F32),
        grid=(s // tm, dff // tf),
        in_specs=[pl.BlockSpec((tm, D_MODEL), lambda i, j: (i, 0)),
                  pl.BlockSpec((D_MODEL, tf), lambda i, j: (0, j)),
                  pl.BlockSpec((D_MODEL, tf), lambda i, j: (0, j)),
                  pl.BlockSpec((tf, D_MODEL), lambda i, j: (j, 0)),
                  _resident((1, D_MODEL), lambda i, j: (0, 0)),
                  _resident((1, D_MODEL), lambda i, j: (0, 0))],
        out_specs=pl.BlockSpec((tm, D_MODEL), lambda i, j: (i, 0)),
        scratch_shapes=[pltpu.VMEM((tm, D_MODEL), BF16), pltpu.VMEM((tm, D_MODEL), F32)],
        name="ffn", compiler_params=_cparams(("parallel", "arbitrary")),
    )(h1, wg, wu, wd, ln_g, ln_b)


def _tile(s, want):
    t = min(s, want)
    assert s % t == 0, (s, t)
    return t


def kernel(x, positions, w_in, ssm_lambda_re, ssm_lambda_im, ssm_log_dt, ssm_b_re, ssm_b_im,
           ssm_c_re, ssm_c_im, ssm_d, w_glu, q_norm_g, w_uq, kv_norm_g, w_ukv, w_out,
           ln1_g, ln1_b, w_ffn_gate, w_ffn_up, w_ffn_down, ln2_g, ln2_b):
    bsz, seq, d_model = x.shape
    assert bsz == 1 and d_model == D_MODEL and w_in.shape[0] == DEPTH
    x2 = x.reshape(seq, D_MODEL)
    pos_row = positions.reshape(1, seq)
    inv_freq = (1.0 / (ROPE_THETA ** (jnp.arange(0, QK_ROPE, 2, dtype=F32) / QK_ROPE))
                ).reshape(HALF_ROPE, 1)
    h = x2
    for l in range(DEPTH):
        lat_hi = SSM_WIDTH + Q_LORA + KV_LORA + QK_ROPE
        w_a = jnp.pad(w_in[l][:, :lat_hi], ((0, 0), (0, LANES - QK_ROPE))).astype(BF16)
        w_g = w_in[l][:, lat_hi:].astype(BF16)
        wq = jnp.pad(w_uq[l].reshape(Q_LORA, N_HEADS, QK_NOPE + QK_ROPE),
                     ((0, 0), (0, 0), (0, QK_PAD - QK_NOPE - QK_ROPE)))
        wqt = wq.reshape(Q_LORA, N_HEADS * QK_PAD).T.astype(BF16)
        wkv = w_ukv[l].reshape(KV_LORA, N_HEADS, QK_NOPE + V_HEAD)
        wk = wkv[:, :, :QK_NOPE].reshape(KV_LORA, N_HEADS * QK_NOPE).astype(BF16)
        wvt = wkv[:, :, QK_NOPE:].reshape(KV_LORA, N_HEADS * V_HEAD).T.astype(BF16)

        u_slabs, c_lat = _inproj_a(h, w_a, _tile(seq, ROWS_INPROJ))
        gates = _inproj_g(h, w_g, _tile(seq, ROWS_INPROJ))

        m_op, p_op, q_op, a_tab = _ssm_prep(ssm_lambda_re[l], ssm_lambda_im[l], ssm_log_dt[l],
                                            ssm_b_re[l], ssm_b_im[l], ssm_c_re[l], ssm_c_im[l])
        d_tab = jnp.tile(ssm_d[l].reshape(N_SLABS, 1, LANES), (1, 1, SSM_T))
        y_slabs = _ssm_scan(u_slabs, m_op, p_op, q_op, a_tab, d_tab, _tile(seq, ROWS_SSM))
        ssm_gated = _glu(y_slabs, w_glu[l].astype(BF16), gates, _tile(seq, ROWS_GLU))

        qt, k, vt = _qkv(c_lat, pos_row, inv_freq, q_norm_g[l].reshape(1, Q_LORA),
                         kv_norm_g[l].reshape(1, KV_LORA), wqt, wk, wvt, _tile(seq, ROWS_QKV))
        mla = _attention(qt, k, vt, _tile(seq, ROWS_ATTN))

        h = _outproj(ssm_gated, gates, mla, h, w_out[l].astype(BF16),
                     ln1_g[l].reshape(1, D_MODEL), ln1_b[l].reshape(1, D_MODEL),
                     _tile(seq, ROWS_OUTPROJ))

        h = _ffn(h, w_ffn_gate[l].astype(BF16), w_ffn_up[l].astype(BF16),
                 w_ffn_down[l].astype(BF16), ln2_g[l].reshape(1, D_MODEL),
                 ln2_b[l].reshape(1, D_MODEL), _tile(seq, ROWS_FFN), COLS_FFN)
    return h.reshape(bsz, seq, D_MODEL)
```

```python
import functools
import math

import jax
import jax.numpy as jnp
from jax import lax
from jax.experimental import pallas as pl
from jax.experimental.pallas import tpu as pltpu

F32 = jnp.float32
BF16 = jnp.bfloat16

D_MODEL = 2048
SSM_GROUP = 16
SSM_WIDTH = D_MODEL // 2
SSM_GROUPS = SSM_WIDTH // SSM_GROUP
SSM_STATE = 64
N_HEADS = 16
QK_NOPE = 128
QK_ROPE = 64
V_HEAD = 128
Q_LORA = 512
KV_LORA = 512
ROPE_THETA = 10000.0
DEPTH = 1
DEEPNORM_ALPHA = (2.0 * DEPTH) ** 0.25
LN_EPS = 1e-5
RMS_EPS = 1e-6

LANES = 128
V7X_VMEM_BYTES = 64 * 1024 * 1024
VMEM_LIMIT = 56 * 1024 * 1024

HALF_ROPE = QK_ROPE // 2
QK_PAD = 256
SSM_T = 8
SLAB_GROUPS = LANES // SSM_GROUP
N_SLABS = SSM_WIDTH // LANES
SLAB_K = SSM_T * LANES
SLAB_STATE = SLAB_GROUPS * 2 * SSM_STATE
HALF_STATE = SLAB_STATE // 2
V_ROWS = V_HEAD + 16

ROWS_INPROJ = 512
ROWS_SSM = 2048
ROWS_GLU = 512
ROWS_QKV = 512
ROWS_ATTN = 512
ROWS_OUTPROJ = 512
ROWS_FFN = 1024
COLS_FFN = 512


def _cparams(sem, vmem=VMEM_LIMIT):
    return pltpu.CompilerParams(dimension_semantics=sem, vmem_limit_bytes=vmem)


def _resident(shape, index_map):
    return pl.BlockSpec(shape, index_map, pipeline_mode=pl.Buffered(1))


def _ssm_prep_kernel(lre_ref, lim_ref, ldt_ref, btr_ref, bti_ref, cr_ref, ci_ref,
                     lre_flat_ref, lim_flat_ref, ldt_flat_ref, m_ref, p_ref, q_ref, a_ref):
    gl, n, pp, t_len = SLAB_GROUPS, SSM_STATE, SSM_GROUP, SSM_T
    lre = lre_ref[...]
    lim = lim_ref[...]
    dt = jnp.exp(ldt_ref[...])

    def apow(xr, xi, k):
        mag = jnp.exp(xr * float(k))
        ang = xi * float(k)
        return mag * jnp.cos(ang), mag * jnp.sin(ang)

    xr = lre * dt
    xi = lim * dt
    ar, ai = apow(xr, xi, 1)
    den = lre * lre + lim * lim
    nr = ar - 1.0
    coef_re = (nr * lre + ai * lim) / den
    coef_im = (ai * lre - nr * lim) / den
    btr = btr_ref[...]
    bti = bti_ref[...]
    bbr = coef_re * btr - coef_im * bti
    bbi = coef_re * bti + coef_im * btr
    cr = cr_ref[...]
    ci = ci_ref[...]

    def spread(width, period):
        r = lax.broadcasted_iota(jnp.int32, (period, width), 0)
        c = lax.broadcasted_iota(jnp.int32, (period, width), 1)
        return (c % period == r).astype(BF16)

    def same_group(rows, row_period, cols, col_period):
        r = lax.broadcasted_iota(jnp.int32, (rows, cols), 0)
        c = lax.broadcasted_iota(jnp.int32, (rows, cols), 1)
        return r // row_period == c // col_period

    rep_p = spread(LANES, pp)
    rep_n = spread(HALF_STATE, n)
    mask_pp = same_group(LANES, pp, LANES, pp)
    mask_pn = same_group(LANES, pp, HALF_STATE, n)

    def block_diag(x, rep, mask):
        x2 = x.reshape(gl * pp, x.shape[-1]).astype(BF16)
        return jnp.where(mask, jnp.dot(x2, rep, preferred_element_type=F32), 0.0)

    m_ref[...] = jnp.zeros_like(m_ref)
    dn = (((2,), (2,)), ((0,), (0,)))
    for k in range(t_len + 1):
        pr, pi = (jnp.ones_like(xr), jnp.zeros_like(xr)) if k == 0 else apow(xr, xi, k)
        car = cr * pr - ci * pi
        cai = cr * pi + ci * pr
        if k < t_len:
            resp = (lax.dot_general(bbr, car, dn, precision=lax.Precision.HIGHEST,
                                    preferred_element_type=F32)
                    - lax.dot_general(bbi, cai, dn, precision=lax.Precision.HIGHEST,
                                      preferred_element_type=F32))
            tile = block_diag(resp, rep_p, mask_pp).astype(BF16)
            for t0 in range(t_len - k):
                m_ref[0, t0 * LANES:(t0 + 1) * LANES, (t0 + k) * LANES:(t0 + k + 1) * LANES] = tile
            t0 = t_len - 1 - k
            for ri, inj in enumerate((pr * bbr - pi * bbi, pr * bbi + pi * bbr)):
                p_ref[0, t0 * LANES:(t0 + 1) * LANES, ri * HALF_STATE:(ri + 1) * HALF_STATE] = (
                    block_diag(inj, rep_n, mask_pn).astype(BF16))
        if k >= 1:
            for ri, ca in enumerate((car, -cai)):
                z = block_diag(ca, rep_n, mask_pn)
                q_ref[0, ri * HALF_STATE:(ri + 1) * HALF_STATE, (k - 1) * LANES:k * LANES] = (
                    z.T.astype(BF16))

    dtf = jnp.exp(ldt_flat_ref[0])
    xrf = lre_flat_ref[0] * dtf
    xif = lim_flat_ref[0] * dtf
    for j in range(1, t_len + 1):
        pr, pi = apow(xrf, xif, t_len * j)
        a_ref[0, j - 1:j, 0:HALF_STATE] = pr
        a_ref[0, j - 1:j, HALF_STATE:] = pi


def _ssm_prep(lam_re, lam_im, log_dt, b_re, b_im, c_re, c_im):
    g, n, p, s = SSM_GROUPS, SSM_STATE, SSM_GROUP, N_SLABS
    gl = SLAB_GROUPS
    grp = lambda shape: pl.BlockSpec((gl,) + shape, lambda i: (i, 0, 0))
    flat = pl.BlockSpec((1, 1, HALF_STATE), lambda i: (i, 0, 0))
    op = pl.BlockSpec((1, SLAB_K, SLAB_K), lambda i: (i, 0, 0))
    return pl.pallas_call(
        _ssm_prep_kernel,
        out_shape=(jax.ShapeDtypeStruct((s, SLAB_K, SLAB_K), BF16),
                   jax.ShapeDtypeStruct((s, SLAB_K, SLAB_STATE), BF16),
                   jax.ShapeDtypeStruct((s, SLAB_STATE, SLAB_K), BF16),
                   jax.ShapeDtypeStruct((s, SSM_T, SLAB_STATE), F32)),
        grid=(s,),
        in_specs=[grp((1, n)), grp((1, n)), grp((1, 1)), grp((p, n)), grp((p, n)),
                  grp((p, n)), grp((p, n)), flat, flat, flat],
        out_specs=(op, op, op, pl.BlockSpec((1, SSM_T, SLAB_STATE), lambda i: (i, 0, 0))),
        name="ssm_prep", compiler_params=_cparams(("parallel",)),
    )(lam_re.reshape(g, 1, n), lam_im.reshape(g, 1, n), log_dt.reshape(g, 1, 1),
      jnp.swapaxes(b_re, 1, 2), jnp.swapaxes(b_im, 1, 2), c_re, c_im,
      lam_re.reshape(s, 1, HALF_STATE), lam_im.reshape(s, 1, HALF_STATE),
      jnp.repeat(log_dt, n).reshape(s, 1, HALF_STATE))


def _inproj_a_kernel(x_ref, w_ref, u_ref, c_ref):
    z = jnp.dot(x_ref[...].astype(BF16), w_ref[...], preferred_element_type=F32)
    for k in range(N_SLABS):
        u_ref[k] = z[:, k * LANES:(k + 1) * LANES]
    c_ref[...] = z[:, SSM_WIDTH:]


def _inproj_a(x2, w_a, tm):
    s = x2.shape[0]
    nc = w_a.shape[1] - SSM_WIDTH
    return pl.pallas_call(
        _inproj_a_kernel,
        out_shape=(jax.ShapeDtypeStruct((N_SLABS, s, LANES), F32),
                   jax.ShapeDtypeStruct((s, nc), F32)),
        grid=(s // tm,),
        in_specs=[pl.BlockSpec((tm, D_MODEL), lambda i: (i, 0)),
                  _resident(w_a.shape, lambda i: (0, 0))],
        out_specs=(pl.BlockSpec((N_SLABS, tm, LANES), lambda i: (0, i, 0)),
                   pl.BlockSpec((tm, nc), lambda i: (i, 0))),
        name="inproj_a", compiler_params=_cparams(("parallel",)),
    )(x2, w_a)


def _inproj_g_kernel(x_ref, w_ref, g_ref):
    z = jnp.dot(x_ref[...].astype(BF16), w_ref[...], preferred_element_type=F32)
    g_ref[...] = jax.nn.sigmoid(z).astype(BF16)


def _inproj_g(x2, w_g, tm):
    s = x2.shape[0]
    ng = w_g.shape[1]
    return pl.pallas_call(
        _inproj_g_kernel,
        out_shape=jax.ShapeDtypeStruct((s, ng), BF16),
        grid=(s // tm,),
        in_specs=[pl.BlockSpec((tm, D_MODEL), lambda i: (i, 0)),
                  _resident(w_g.shape, lambda i: (0, 0))],
        out_specs=pl.BlockSpec((tm, ng), lambda i: (i, 0)),
        name="inproj_g", compiler_params=_cparams(("parallel",)),
    )(x2, w_g)


def _cmul(ar, ai, br, bi):
    return ar * br - ai * bi, ar * bi + ai * br


def _ssm_scan_kernel(u_ref, m_ref, p_ref, q_ref, a_ref, d_ref, y_ref, hs_ref, carry_ref, *, rows):
    tb = pl.program_id(1)

    @pl.when(tb == 0)
    def _():
        carry_ref[...] = jnp.zeros_like(carry_ref)

    v32 = jnp.concatenate(
        [u_ref[0, pl.ds(t, rows, stride=SSM_T), :] for t in range(SSM_T)], axis=-1)
    vb = v32.astype(BF16)
    x_inj = jnp.dot(vb, p_ref[0], preferred_element_type=F32)
    y_intra = jnp.dot(vb, m_ref[0], preferred_element_type=F32)

    a_tab = a_ref[0]
    row = lax.broadcasted_iota(jnp.int32, (rows, HALF_STATE), 0)
    xs = pltpu.roll(x_inj, 1, 0)
    first = row == 0
    tiles = rows // SSM_T
    re = jnp.where(first, carry_ref[:, :HALF_STATE][0:1], xs[:, :HALF_STATE]).reshape(
        tiles, SSM_T, HALF_STATE)
    im = jnp.where(first, carry_ref[:, HALF_STATE:][0:1], xs[:, HALF_STATE:]).reshape(
        tiles, SSM_T, HALF_STATE)
    sub = lax.broadcasted_iota(jnp.int32, (SSM_T, HALF_STATE), 0)
    for d in (1, 2, 4):
        ar = jnp.where(sub >= d, a_tab[d - 1:d, :HALF_STATE], 0.0)
        ai = jnp.where(sub >= d, a_tab[d - 1:d, HALF_STATE:], 0.0)
        pr, pi = _cmul(ar, ai, pltpu.roll(re, d, 1), pltpu.roll(im, d, 1))
        re = re + pr
        im = im + pi
    hs_ref[:, :HALF_STATE] = re.reshape(rows, HALF_STATE)
    hs_ref[:, HALF_STATE:] = im.reshape(rows, HALF_STATE)

    tab_r = a_tab[:, :HALF_STATE]
    tab_i = a_tab[:, HALF_STATE:]

    def tile_body(k, last):
        lr, li = last
        r0 = pl.multiple_of(k * SSM_T, SSM_T)
        cr, ci = _cmul(tab_r, tab_i, lr, li)
        hr = hs_ref[pl.ds(r0, SSM_T), :HALF_STATE] + cr
        hi = hs_ref[pl.ds(r0, SSM_T), HALF_STATE:] + ci
        hs_ref[pl.ds(r0, SSM_T), :HALF_STATE] = hr
        hs_ref[pl.ds(r0, SSM_T), HALF_STATE:] = hi
        return (jnp.broadcast_to(hr[SSM_T - 1:SSM_T], (SSM_T, HALF_STATE)),
                jnp.broadcast_to(hi[SSM_T - 1:SSM_T], (SSM_T, HALF_STATE)))

    zero = jnp.zeros((SSM_T, HALF_STATE), F32)
    lr, li = lax.fori_loop(0, rows // SSM_T, tile_body, (zero, zero))

    nr, ni = _cmul(a_tab[0:1, :HALF_STATE], a_tab[0:1, HALF_STATE:], lr, li)
    x_last = x_inj[rows - 1:rows, :]
    carry_ref[:, :HALF_STATE] = nr + x_last[:, :HALF_STATE]
    carry_ref[:, HALF_STATE:] = ni + x_last[:, HALF_STATE:]

    y = (y_intra + jnp.dot(hs_ref[...].astype(BF16), q_ref[0], preferred_element_type=F32)
         + d_ref[0] * v32)
    for t in range(SSM_T):
        y_ref[0, pl.ds(t, rows, stride=SSM_T), :] = y[:, t * LANES:(t + 1) * LANES]


def _ssm_scan(u_slabs, m_op, p_op, q_op, a_tab, d_tab, tb_rows):
    s = u_slabs.shape[1]
    rows = tb_rows // SSM_T
    op_spec = pl.BlockSpec((1, SLAB_K, SLAB_K), lambda i, j: (i, 0, 0))
    return pl.pallas_call(
        functools.partial(_ssm_scan_kernel, rows=rows),
        out_shape=jax.ShapeDtypeStruct((N_SLABS, s, LANES), F32),
        grid=(N_SLABS, s // tb_rows),
        in_specs=[pl.BlockSpec((1, tb_rows, LANES), lambda i, j: (i, j, 0)),
                  op_spec, op_spec, op_spec,
                  pl.BlockSpec((1, SSM_T, SLAB_STATE), lambda i, j: (i, 0, 0)),
                  pl.BlockSpec((1, 1, SLAB_K), lambda i, j: (i, 0, 0))],
        out_specs=pl.BlockSpec((1, tb_rows, LANES), lambda i, j: (i, j, 0)),
        scratch_shapes=[pltpu.VMEM((rows, SLAB_STATE), F32),
                        pltpu.VMEM((SSM_T, SLAB_STATE), F32)],
        name="ssm_scan", compiler_params=_cparams(("parallel", "arbitrary")),
    )(u_slabs, m_op, p_op, q_op, a_tab, d_tab)


def _gelu_tanh(x):
    c = math.sqrt(2.0 / math.pi)
    return 0.5 * x * (1.0 + jnp.tanh(c * (x + 0.044715 * (x * x * x))))


def _glu_kernel(y_ref, w_ref, g_ref, o_ref):
    y = jnp.concatenate([y_ref[k] for k in range(N_SLABS)], axis=-1)
    z = jnp.dot(_gelu_tanh(y).astype(BF16), w_ref[...], preferred_element_type=F32)
    out = z[:, :D_MODEL] * jax.nn.sigmoid(z[:, D_MODEL:]) * g_ref[...].astype(F32)
    o_ref[...] = out.astype(BF16)


def _glu(y_slabs, w_glu, gates, tm):
    s = y_slabs.shape[1]
    return pl.pallas_call(
        _glu_kernel,
        out_shape=jax.ShapeDtypeStruct((s, D_MODEL), BF16),
        grid=(s // tm,),
        in_specs=[pl.BlockSpec((N_SLABS, tm, LANES), lambda i: (0, i, 0)),
                  _resident(w_glu.shape, lambda i: (0, 0)),
                  pl.BlockSpec((tm, D_MODEL), lambda i: (i, 0))],
        out_specs=pl.BlockSpec((tm, D_MODEL), lambda i: (i, 0)),
        name="glu", compiler_params=_cparams(("parallel",)),
    )(y_slabs, w_glu, gates)


def _rms(x, g):
    return x * lax.rsqrt(jnp.mean(x * x, axis=-1, keepdims=True) + RMS_EPS) * g


_NT = (((1,), (1,)), ((), ()))


def _qkv_kernel(cq_ref, ckv_ref, kr_ref, pos_ref, invf_ref, gq_ref, gkv_ref,
                wqt_ref, wk_ref, wvt_ref, qt_ref, k_ref, vt_ref, *, scale):
    tm = cq_ref.shape[0]
    cqn = _rms(cq_ref[...], gq_ref[...]).astype(BF16)
    ckvn = _rms(ckv_ref[...], gkv_ref[...]).astype(BF16)

    ang = invf_ref[...] * pos_ref[...].astype(F32)
    cos_t = jnp.cos(ang)
    sin_t = jnp.sin(ang)

    qt = lax.dot_general(wqt_ref[...], cqn, _NT, preferred_element_type=F32) * scale
    rope_hi = QK_NOPE + QK_ROPE
    for h in range(N_HEADS):
        b = h * QK_PAD
        t1 = qt[b + QK_NOPE:b + QK_NOPE + HALF_ROPE]
        t2 = qt[b + QK_NOPE + HALF_ROPE:b + rope_hi]
        qt_ref[h, 0:QK_NOPE, :] = qt[b:b + QK_NOPE].astype(BF16)
        qt_ref[h, QK_NOPE:QK_NOPE + HALF_ROPE, :] = (t1 * cos_t - t2 * sin_t).astype(BF16)
        qt_ref[h, QK_NOPE + HALF_ROPE:rope_hi, :] = (t1 * sin_t + t2 * cos_t).astype(BF16)
        qt_ref[h, rope_hi:QK_PAD, :] = jnp.zeros((QK_PAD - rope_hi, tm), BF16)

    kr_t = kr_ref[...].T
    k1 = kr_t[0:HALF_ROPE]
    k2 = kr_t[HALF_ROPE:QK_ROPE]
    krot_t = jnp.concatenate(
        [k1 * cos_t - k2 * sin_t, k1 * sin_t + k2 * cos_t,
         jnp.zeros((LANES - QK_ROPE, tm), F32)], axis=0)
    krot = krot_t.T.astype(BF16)

    kn = jnp.dot(ckvn, wk_ref[...], preferred_element_type=F32)
    vt = lax.dot_general(wvt_ref[...], ckvn, _NT, preferred_element_type=F32)
    for h in range(N_HEADS):
        k_ref[h, :, 0:QK_NOPE] = kn[:, h * QK_NOPE:(h + 1) * QK_NOPE].astype(BF16)
        k_ref[h, :, QK_NOPE:QK_PAD] = krot
        vt_ref[h, 0:V_HEAD, :] = vt[h * V_HEAD:(h + 1) * V_HEAD].astype(BF16)
        vt_ref[h, V_HEAD:V_ROWS, :] = jnp.ones((V_ROWS - V_HEAD, tm), BF16)


def _qkv(c_lat, pos_row, inv_freq, gq, gkv, wqt, wk, wvt, tm):
    s = c_lat.shape[0]
    scale = math.log2(math.e) / math.sqrt(QK_NOPE + QK_ROPE)
    ncq = Q_LORA // Q_LORA
    return pl.pallas_call(
        functools.partial(_qkv_kernel, scale=scale),
        out_shape=(jax.ShapeDtypeStruct((N_HEADS, QK_PAD, s), BF16),
                   jax.ShapeDtypeStruct((N_HEADS, s, QK_PAD), BF16),
                   jax.ShapeDtypeStruct((N_HEADS, V_ROWS, s), BF16)),
        grid=(s // tm,),
        in_specs=[pl.BlockSpec((tm, Q_LORA), lambda i: (i, 0)),
                  pl.BlockSpec((tm, KV_LORA), lambda i: (i, ncq)),
                  pl.BlockSpec((tm, LANES), lambda i: (i, (Q_LORA + KV_LORA) // LANES)),
                  pl.BlockSpec((1, tm), lambda i: (0, i)),
                  _resident((HALF_ROPE, 1), lambda i: (0, 0)),
                  _resident((1, Q_LORA), lambda i: (0, 0)),
                  _resident((1, KV_LORA), lambda i: (0, 0)),
                  _resident(wqt.shape, lambda i: (0, 0)),
                  _resident(wk.shape, lambda i: (0, 0)),
                  _resident(wvt.shape, lambda i: (0, 0))],
        out_specs=(pl.BlockSpec((N_HEADS, QK_PAD, tm), lambda i: (0, 0, i)),
                   pl.BlockSpec((N_HEADS, tm, QK_PAD), lambda i: (0, i, 0)),
                   pl.BlockSpec((N_HEADS, V_ROWS, tm), lambda i: (0, 0, i))),
        name="qkv", compiler_params=_cparams(("parallel",)),
    )(c_lat, c_lat, c_lat, pos_row, inv_freq, gq, gkv, wqt, wk, wvt)


_NEG = -1e30


ATTN_UNROLL = 8


def _attn_kernel(qt_ref, k_ref, vt_ref, o_ref, s_ref, mx_ref, acc_ref, m_ref, *, t):
    seq = k_ref.shape[1]
    tq = 2 * t
    krow = lax.broadcasted_iota(jnp.int32, (t, t), 0)
    qcol = lax.broadcasted_iota(jnp.int32, (t, t), 1)
    causal = krow <= qcol

    def pv(ks, p):
        return jnp.dot(vt_ref[0, :, pl.ds(ks, t)], p, preferred_element_type=F32)

    def query_tile(i, carry):
        kd = pl.multiple_of(i * tq, tq)
        n_full = 2 * i

        def qk(ks, q0, nq):
            return jnp.dot(k_ref[0, pl.ds(ks, t), :], qt_ref[0, :, pl.ds(kd + q0, nq)],
                           preferred_element_type=F32)

        def put_scores(slot, ks):
            s = qk(ks, 0, tq)
            s_ref[slot] = s
            mx_ref[slot] = jnp.max(s, axis=0, keepdims=True)

        put_scores(0, 0)
        s0 = qk(kd, 0, tq)
        s0 = jnp.concatenate([jnp.where(causal, s0[:, :t], _NEG), s0[:, t:]], axis=1)
        m0 = jnp.max(s0, axis=0, keepdims=True)
        acc0 = pv(kd, jnp.exp2(s0 - m0).astype(BF16))
        s1 = jnp.where(causal, qk(kd + t, t, t), _NEG)
        m0r = m0[:, t:]
        m1 = jnp.maximum(m0r, jnp.max(s1, axis=0, keepdims=True))
        acc1 = jnp.exp2(m0r - m1) * acc0[:, t:] + pv(kd + t, jnp.exp2(s1 - m1).astype(BF16))
        m_ref[:, :t] = m0[:, :t]
        m_ref[:, t:] = m1
        acc_ref[:, :t] = acc0[:, :t]
        acc_ref[:, t:] = acc1

        def item(slot, j):
            put_scores(1 - slot, pl.multiple_of(jnp.minimum(j + 1, jnp.maximum(n_full - 1, 0)) * t, t))
            m_old = m_ref[...]
            m_new = jnp.maximum(m_old, mx_ref[slot])
            p = jnp.exp2(s_ref[slot] - m_new).astype(BF16)
            acc_ref[...] = jnp.exp2(m_old - m_new) * acc_ref[...] + pv(pl.multiple_of(j * t, t), p)
            m_ref[...] = m_new

        def run(width, j0):
            for u in range(width):
                item(u % 2, j0 + u)

        done = 0
        width = 2
        while width < ATTN_UNROLL:
            bit = lax.rem(n_full // width, 2)
            pl.when(bit == 1)(functools.partial(run, width, done))
            done = done + bit * width
            width *= 2

        def main(r, c):
            run(ATTN_UNROLL, done + ATTN_UNROLL * r)
            return c

        lax.fori_loop(0, n_full // ATTN_UNROLL, main, 0)

        acc = acc_ref[...]
        o_ref[pl.ds(kd, tq), :] = (acc[:V_HEAD] * (1.0 / acc[V_HEAD:V_HEAD + 1])).T.astype(BF16)
        return carry

    lax.fori_loop(0, seq // tq, query_tile, 0)


def _attention(qt, k, vt, t):
    s = k.shape[1]
    head = lambda h: (h, 0, 0)
    return pl.pallas_call(
        functools.partial(_attn_kernel, t=t),
        out_shape=jax.ShapeDtypeStruct((s, N_HEADS * V_HEAD), BF16),
        grid=(N_HEADS,),
        in_specs=[pl.BlockSpec((1, QK_PAD, s), head),
                  pl.BlockSpec((1, s, QK_PAD), head),
                  pl.BlockSpec((1, V_ROWS, s), head)],
        out_specs=pl.BlockSpec((s, V_HEAD), lambda h: (0, h), pipeline_mode=pl.Buffered(1)),
        scratch_shapes=[pltpu.VMEM((2, t, 2 * t), F32), pltpu.VMEM((2, 1, 2 * t), F32),
                        pltpu.VMEM((V_ROWS, 2 * t), F32), pltpu.VMEM((1, 2 * t), F32)],
        name="attn", compiler_params=_cparams(("parallel",)),
    )(qt, k, vt)


def _layer_norm(r, g, b):
    mu = jnp.mean(r, axis=-1, keepdims=True)
    c = r - mu
    var = jnp.mean(c * c, axis=-1, keepdims=True)
    return c * lax.rsqrt(var + LN_EPS) * g + b


def _outproj_kernel(sg_ref, gm_ref, mla_ref, x_ref, w_ref, g_ref, b_ref, h_ref, hb_ref):
    merged = sg_ref[...].astype(F32) + gm_ref[...].astype(F32) * mla_ref[...].astype(F32)
    mix = jnp.dot(merged.astype(BF16), w_ref[...], preferred_element_type=F32)
    h = _layer_norm(DEEPNORM_ALPHA * x_ref[...] + mix, g_ref[...], b_ref[...])
    h_ref[...] = h
    hb_ref[...] = h.astype(BF16)


def _outproj(ssm_gated, gates, mla, x2, w_out, ln_g, ln_b, tm):
    s = x2.shape[0]
    row = lambda i: (i, 0)
    return pl.pallas_call(
        _outproj_kernel,
        out_shape=(jax.ShapeDtypeStruct((s, D_MODEL), F32),
                   jax.ShapeDtypeStruct((s, D_MODEL), BF16)),
        grid=(s // tm,),
        in_specs=[pl.BlockSpec((tm, D_MODEL), row),
                  pl.BlockSpec((tm, D_MODEL), lambda i: (i, 1)),
                  pl.BlockSpec((tm, D_MODEL), row),
                  pl.BlockSpec((tm, D_MODEL), row),
                  _resident(w_out.shape, lambda i: (0, 0)),
                  _resident((1, D_MODEL), lambda i: (0, 0)),
                  _resident((1, D_MODEL), lambda i: (0, 0))],
        out_specs=(pl.BlockSpec((tm, D_MODEL), row), pl.BlockSpec((tm, D_MODEL), row)),
        name="outproj", compiler_params=_cparams(("parallel",)),
    )(ssm_gated, gates, mla, x2, w_out, ln_g, ln_b)


FFN_RESIDUAL_CHUNKS = 8


def _ffn_kernel(hb_ref, h_ref, wg_ref, wu_ref, wd_ref, g_ref, b_ref, o_ref):
    j = pl.program_id(1)
    chunk = h_ref.shape[0]

    @pl.when(j == 0)
    def _():
        o_ref[...] = jnp.zeros_like(o_ref)

    hb = hb_ref[...]
    gate = jnp.dot(hb, wg_ref[...], preferred_element_type=F32)
    up = jnp.dot(hb, wu_ref[...], preferred_element_type=F32)
    act = (gate * jax.nn.sigmoid(gate) * up).astype(BF16)
    o_ref[...] += jnp.dot(act, wd_ref[...], preferred_element_type=F32)

    @pl.when(j < FFN_RESIDUAL_CHUNKS)
    def _():
        rows = pl.ds(pl.multiple_of(j * chunk, chunk), chunk)
        o_ref[rows, :] += DEEPNORM_ALPHA * h_ref[...]

    @pl.when(j == pl.num_programs(1) - 1)
    def _():
        o_ref[...] = _layer_norm(o_ref[...], g_ref[...], b_ref[...])


def _ffn(hb, h1, wg, wu, wd, ln_g, ln_b, tm, tf):
    s = h1.shape[0]
    dff = wg.shape[1]
    steps = dff // tf
    assert steps >= FFN_RESIDUAL_CHUNKS and tm % FFN_RESIDUAL_CHUNKS == 0
    chunk = tm // FFN_RESIDUAL_CHUNKS
    return pl.pallas_call(
        _ffn_kernel,
        out_shape=jax.ShapeDtypeStruct((s, D_MODEL), F32),
        grid=(s // tm, steps),
        in_specs=[pl.BlockSpec((tm, D_MODEL), lambda i, j: (i, 0)),
                  pl.BlockSpec((chunk, D_MODEL),
                               lambda i, j: (i * FFN_RESIDUAL_CHUNKS
                                             + jnp.minimum(j, FFN_RESIDUAL_CHUNKS - 1), 0)),
                  pl.BlockSpec((D_MODEL, tf), lambda i, j: (0, j)),
                  pl.BlockSpec((D_MODEL, tf), lambda i, j: (0, j)),
                  pl.BlockSpec((tf, D_MODEL), lambda i, j: (j, 0)),
                  _resident((1, D_MODEL), lambda i, j: (0, 0)),
                  _resident((1, D_MODEL), lambda i, j: (0, 0))],
        out_specs=pl.BlockSpec((tm, D_MODEL), lambda i, j: (i, 0)),
        name="ffn", compiler_params=_cparams(("parallel", "arbitrary")),
    )(hb, h1, wg, wu, wd, ln_g, ln_b)


def _tile(s, want):
    t = min(s, want)
    assert s % t == 0, (s, t)
    return t


def kernel(x, positions, w_in, ssm_lambda_re, ssm_lambda_im, ssm_log_dt, ssm_b_re, ssm_b_im,
           ssm_c_re, ssm_c_im, ssm_d, w_glu, q_norm_g, w_uq, kv_norm_g, w_ukv, w_out,
           ln1_g, ln1_b, w_ffn_gate, w_ffn_up, w_ffn_down, ln2_g, ln2_b):
    bsz, seq, d_model = x.shape
    assert bsz == 1 and d_model == D_MODEL and w_in.shape[0] == DEPTH
    x2 = x.reshape(seq, D_MODEL)
    pos_row = positions.reshape(1, seq)
    inv_freq = (1.0 / (ROPE_THETA ** (jnp.arange(0, QK_ROPE, 2, dtype=F32) / QK_ROPE))
                ).reshape(HALF_ROPE, 1)
    h = x2
    for l in range(DEPTH):
        lat_hi = SSM_WIDTH + Q_LORA + KV_LORA + QK_ROPE
        w_a = jnp.pad(w_in[l][:, :lat_hi], ((0, 0), (0, LANES - QK_ROPE))).astype(BF16)
        w_g = w_in[l][:, lat_hi:].astype(BF16)
        wq = jnp.pad(w_uq[l].reshape(Q_LORA, N_HEADS, QK_NOPE + QK_ROPE),
                     ((0, 0), (0, 0), (0, QK_PAD - QK_NOPE - QK_ROPE)))
        wqt = wq.reshape(Q_LORA, N_HEADS * QK_PAD).T.astype(BF16)
        wkv = w_ukv[l].reshape(KV_LORA, N_HEADS, QK_NOPE + V_HEAD)
        wk = wkv[:, :, :QK_NOPE].reshape(KV_LORA, N_HEADS * QK_NOPE).astype(BF16)
        wvt = wkv[:, :, QK_NOPE:].reshape(KV_LORA, N_HEADS * V_HEAD).T.astype(BF16)

        u_slabs, c_lat = _inproj_a(h, w_a, _tile(seq, ROWS_INPROJ))
        gates = _inproj_g(h, w_g, _tile(seq, ROWS_INPROJ))

        m_op, p_op, q_op, a_tab = _ssm_prep(ssm_lambda_re[l], ssm_lambda_im[l], ssm_log_dt[l],
                                            ssm_b_re[l], ssm_b_im[l], ssm_c_re[l], ssm_c_im[l])
        d_tab = jnp.tile(ssm_d[l].reshape(N_SLABS, 1, LANES), (1, 1, SSM_T))
        y_slabs = _ssm_scan(u_slabs, m_op, p_op, q_op, a_tab, d_tab, _tile(seq, ROWS_SSM))
        ssm_gated = _glu(y_slabs, w_glu[l].astype(BF16), gates, _tile(seq, ROWS_GLU))

        qt, k, vt = _qkv(c_lat, pos_row, inv_freq, q_norm_g[l].reshape(1, Q_LORA),
                         kv_norm_g[l].reshape(1, KV_LORA), wqt, wk, wvt, _tile(seq, ROWS_QKV))
        mla = _attention(qt, k, vt, _tile(seq // 2, ROWS_ATTN))

        h, hb = _outproj(ssm_gated, gates, mla, h, w_out[l].astype(BF16),
                         ln1_g[l].reshape(1, D_MODEL), ln1_b[l].reshape(1, D_MODEL),
                         _tile(seq, ROWS_OUTPROJ))

        h = _ffn(hb, h, w_ffn_gate[l].astype(BF16), w_ffn_up[l].astype(BF16),
                 w_ffn_down[l].astype(BF16), ln2_g[l].reshape(1, D_MODEL),
                 ln2_b[l].reshape(1, D_MODEL), _tile(seq, ROWS_FFN), COLS_FFN)
    return h.reshape(bsz, seq, D_MODEL)
```

```python
import functools
import math

import jax
import jax.numpy as jnp
from jax import lax
from jax.experimental import pallas as pl
from jax.experimental.pallas import tpu as pltpu

F32 = jnp.float32
BF16 = jnp.bfloat16

D_MODEL = 2048
SSM_GROUP = 16
SSM_WIDTH = D_MODEL // 2
SSM_GROUPS = SSM_WIDTH // SSM_GROUP
SSM_STATE = 64
N_HEADS = 16
QK_NOPE = 128
QK_ROPE = 64
V_HEAD = 128
Q_LORA = 512
KV_LORA = 512
ROPE_THETA = 10000.0
DEPTH = 1
DEEPNORM_ALPHA = (2.0 * DEPTH) ** 0.25
LN_EPS = 1e-5
RMS_EPS = 1e-6

LANES = 128
V7X_VMEM_BYTES = 64 * 1024 * 1024
VMEM_LIMIT = 56 * 1024 * 1024

HALF_ROPE = QK_ROPE // 2
QK_PAD = 256
SSM_T = 8
SLAB_GROUPS = LANES // SSM_GROUP
N_SLABS = SSM_WIDTH // LANES
SLAB_K = SSM_T * LANES
SLAB_STATE = SLAB_GROUPS * 2 * SSM_STATE
HALF_STATE = SLAB_STATE // 2
V_ROWS = V_HEAD + 16

ROWS_INPROJ = 512
ROWS_SSM = 2048
ROWS_GLU = 512
ROWS_QKV = 512
ROWS_ATTN = 512
ROWS_OUTPROJ = 512
ROWS_FFN = 1024
COLS_FFN = 512


def _cparams(sem, vmem=VMEM_LIMIT):
    return pltpu.CompilerParams(dimension_semantics=sem, vmem_limit_bytes=vmem)


def _resident(shape, index_map):
    return pl.BlockSpec(shape, index_map, pipeline_mode=pl.Buffered(1))


def _ssm_prep_kernel(lre_ref, lim_ref, ldt_ref, btr_ref, bti_ref, cr_ref, ci_ref,
                     lre_flat_ref, lim_flat_ref, ldt_flat_ref, m_ref, p_ref, q_ref, a_ref):
    gl, n, pp, t_len = SLAB_GROUPS, SSM_STATE, SSM_GROUP, SSM_T
    lre = lre_ref[...]
    lim = lim_ref[...]
    dt = jnp.exp(ldt_ref[...])

    def apow(xr, xi, k):
        mag = jnp.exp(xr * float(k))
        ang = xi * float(k)
        return mag * jnp.cos(ang), mag * jnp.sin(ang)

    xr = lre * dt
    xi = lim * dt
    ar, ai = apow(xr, xi, 1)
    den = lre * lre + lim * lim
    nr = ar - 1.0
    coef_re = (nr * lre + ai * lim) / den
    coef_im = (ai * lre - nr * lim) / den
    btr = btr_ref[...]
    bti = bti_ref[...]
    bbr = coef_re * btr - coef_im * bti
    bbi = coef_re * bti + coef_im * btr
    cr = cr_ref[...]
    ci = ci_ref[...]

    def spread(width, period):
        r = lax.broadcasted_iota(jnp.int32, (period, width), 0)
        c = lax.broadcasted_iota(jnp.int32, (period, width), 1)
        return (c % period == r).astype(BF16)

    def same_group(rows, row_period, cols, col_period):
        r = lax.broadcasted_iota(jnp.int32, (rows, cols), 0)
        c = lax.broadcasted_iota(jnp.int32, (rows, cols), 1)
        return r // row_period == c // col_period

    rep_p = spread(LANES, pp)
    rep_n = spread(HALF_STATE, n)
    mask_pp = same_group(LANES, pp, LANES, pp)
    mask_pn = same_group(LANES, pp, HALF_STATE, n)

    def block_diag(x, rep, mask):
        x2 = x.reshape(gl * pp, x.shape[-1]).astype(BF16)
        return jnp.where(mask, jnp.dot(x2, rep, preferred_element_type=F32), 0.0)

    m_ref[...] = jnp.zeros_like(m_ref)
    dn = (((2,), (2,)), ((0,), (0,)))
    for k in range(t_len + 1):
        pr, pi = (jnp.ones_like(xr), jnp.zeros_like(xr)) if k == 0 else apow(xr, xi, k)
        car = cr * pr - ci * pi
        cai = cr * pi + ci * pr
        if k < t_len:
            resp = (lax.dot_general(bbr, car, dn, precision=lax.Precision.HIGHEST,
                                    preferred_element_type=F32)
                    - lax.dot_general(bbi, cai, dn, precision=lax.Precision.HIGHEST,
                                      preferred_element_type=F32))
            tile = block_diag(resp, rep_p, mask_pp).astype(BF16)
            for t0 in range(t_len - k):
                m_ref[0, t0 * LANES:(t0 + 1) * LANES, (t0 + k) * LANES:(t0 + k + 1) * LANES] = tile
            t0 = t_len - 1 - k
            for ri, inj in enumerate((pr * bbr - pi * bbi, pr * bbi + pi * bbr)):
                p_ref[0, t0 * LANES:(t0 + 1) * LANES, ri * HALF_STATE:(ri + 1) * HALF_STATE] = (
                    block_diag(inj, rep_n, mask_pn).astype(BF16))
        if k >= 1:
            for ri, ca in enumerate((car, -cai)):
                z = block_diag(ca, rep_n, mask_pn)
                q_ref[0, ri * HALF_STATE:(ri + 1) * HALF_STATE, (k - 1) * LANES:k * LANES] = (
                    z.T.astype(BF16))

    dtf = jnp.exp(ldt_flat_ref[0])
    xrf = lre_flat_ref[0] * dtf
    xif = lim_flat_ref[0] * dtf
    for j in range(1, t_len + 1):
        pr, pi = apow(xrf, xif, t_len * j)
        a_ref[0, j - 1:j, 0:HALF_STATE] = pr
        a_ref[0, j - 1:j, HALF_STATE:] = pi


def _ssm_prep(lam_re, lam_im, log_dt, b_re, b_im, c_re, c_im):
    g, n, p, s = SSM_GROUPS, SSM_STATE, SSM_GROUP, N_SLABS
    gl = SLAB_GROUPS
    grp = lambda shape: pl.BlockSpec((gl,) + shape, lambda i: (i, 0, 0))
    flat = pl.BlockSpec((1, 1, HALF_STATE), lambda i: (i, 0, 0))
    op = pl.BlockSpec((1, SLAB_K, SLAB_K), lambda i: (i, 0, 0))
    return pl.pallas_call(
        _ssm_prep_kernel,
        out_shape=(jax.ShapeDtypeStruct((s, SLAB_K, SLAB_K), BF16),
                   jax.ShapeDtypeStruct((s, SLAB_K, SLAB_STATE), BF16),
                   jax.ShapeDtypeStruct((s, SLAB_STATE, SLAB_K), BF16),
                   jax.ShapeDtypeStruct((s, SSM_T, SLAB_STATE), F32)),
        grid=(s,),
        in_specs=[grp((1, n)), grp((1, n)), grp((1, 1)), grp((p, n)), grp((p, n)),
                  grp((p, n)), grp((p, n)), flat, flat, flat],
        out_specs=(op, op, op, pl.BlockSpec((1, SSM_T, SLAB_STATE), lambda i: (i, 0, 0))),
        name="ssm_prep", compiler_params=_cparams(("parallel",)),
    )(lam_re.reshape(g, 1, n), lam_im.reshape(g, 1, n), log_dt.reshape(g, 1, 1),
      jnp.swapaxes(b_re, 1, 2), jnp.swapaxes(b_im, 1, 2), c_re, c_im,
      lam_re.reshape(s, 1, HALF_STATE), lam_im.reshape(s, 1, HALF_STATE),
      jnp.repeat(log_dt, n).reshape(s, 1, HALF_STATE))


def _inproj_a_kernel(x_ref, w_ref, u_ref, c_ref):
    z = jnp.dot(x_ref[...].astype(BF16), w_ref[...], preferred_element_type=F32)
    for k in range(N_SLABS):
        u_ref[k] = z[:, k * LANES:(k + 1) * LANES]
    c_ref[...] = z[:, SSM_WIDTH:]


def _inproj_a(x2, w_a, tm):
    s = x2.shape[0]
    nc = w_a.shape[1] - SSM_WIDTH
    return pl.pallas_call(
        _inproj_a_kernel,
        out_shape=(jax.ShapeDtypeStruct((N_SLABS, s, LANES), F32),
                   jax.ShapeDtypeStruct((s, nc), F32)),
        grid=(s // tm,),
        in_specs=[pl.BlockSpec((tm, D_MODEL), lambda i: (i, 0)),
                  _resident(w_a.shape, lambda i: (0, 0))],
        out_specs=(pl.BlockSpec((N_SLABS, tm, LANES), lambda i: (0, i, 0)),
                   pl.BlockSpec((tm, nc), lambda i: (i, 0))),
        name="inproj_a", compiler_params=_cparams(("parallel",)),
    )(x2, w_a)


def _inproj_g_kernel(x_ref, w_ref, g_ref):
    z = jnp.dot(x_ref[...].astype(BF16), w_ref[...], preferred_element_type=F32)
    g_ref[...] = jax.nn.sigmoid(z).astype(BF16)


def _inproj_g(x2, w_g, tm):
    s = x2.shape[0]
    ng = w_g.shape[1]
    return pl.pallas_call(
        _inproj_g_kernel,
        out_shape=jax.ShapeDtypeStruct((s, ng), BF16),
        grid=(s // tm,),
        in_specs=[pl.BlockSpec((tm, D_MODEL), lambda i: (i, 0)),
                  _resident(w_g.shape, lambda i: (0, 0))],
        out_specs=pl.BlockSpec((tm, ng), lambda i: (i, 0)),
        name="inproj_g", compiler_params=_cparams(("parallel",)),
    )(x2, w_g)


def _cmul(ar, ai, br, bi):
    return ar * br - ai * bi, ar * bi + ai * br


def _ssm_scan_kernel(u_ref, m_ref, p_ref, q_ref, a_ref, d_ref, y_ref, hs_ref, carry_ref, *, rows):
    tb = pl.program_id(1)

    @pl.when(tb == 0)
    def _():
        carry_ref[...] = jnp.zeros_like(carry_ref)

    v32 = jnp.concatenate(
        [u_ref[0, pl.ds(t, rows, stride=SSM_T), :] for t in range(SSM_T)], axis=-1)
    vb = v32.astype(BF16)
    x_inj = jnp.dot(vb, p_ref[0], preferred_element_type=F32)
    y_intra = jnp.dot(vb, m_ref[0], preferred_element_type=F32)

    a_tab = a_ref[0]
    row = lax.broadcasted_iota(jnp.int32, (rows, HALF_STATE), 0)
    xs = pltpu.roll(x_inj, 1, 0)
    first = row == 0
    tiles = rows // SSM_T
    re = jnp.where(first, carry_ref[:, :HALF_STATE][0:1], xs[:, :HALF_STATE]).reshape(
        tiles, SSM_T, HALF_STATE)
    im = jnp.where(first, carry_ref[:, HALF_STATE:][0:1], xs[:, HALF_STATE:]).reshape(
        tiles, SSM_T, HALF_STATE)
    sub = lax.broadcasted_iota(jnp.int32, (SSM_T, HALF_STATE), 0)
    for d in (1, 2, 4):
        ar = jnp.where(sub >= d, a_tab[d - 1:d, :HALF_STATE], 0.0)
        ai = jnp.where(sub >= d, a_tab[d - 1:d, HALF_STATE:], 0.0)
        pr, pi = _cmul(ar, ai, pltpu.roll(re, d, 1), pltpu.roll(im, d, 1))
        re = re + pr
        im = im + pi
    hs_ref[:, :HALF_STATE] = re.reshape(rows, HALF_STATE)
    hs_ref[:, HALF_STATE:] = im.reshape(rows, HALF_STATE)

    tab_r = a_tab[:, :HALF_STATE]
    tab_i = a_tab[:, HALF_STATE:]

    def tile_body(k, last):
        lr, li = last
        r0 = pl.multiple_of(k * SSM_T, SSM_T)
        cr, ci = _cmul(tab_r, tab_i, lr, li)
        hr = hs_ref[pl.ds(r0, SSM_T), :HALF_STATE] + cr
        hi = hs_ref[pl.ds(r0, SSM_T), HALF_STATE:] + ci
        hs_ref[pl.ds(r0, SSM_T), :HALF_STATE] = hr
        hs_ref[pl.ds(r0, SSM_T), HALF_STATE:] = hi
        return (jnp.broadcast_to(hr[SSM_T - 1:SSM_T], (SSM_T, HALF_STATE)),
                jnp.broadcast_to(hi[SSM_T - 1:SSM_T], (SSM_T, HALF_STATE)))

    zero = jnp.zeros((SSM_T, HALF_STATE), F32)
    lr, li = lax.fori_loop(0, rows // SSM_T, tile_body, (zero, zero))

    nr, ni = _cmul(a_tab[0:1, :HALF_STATE], a_tab[0:1, HALF_STATE:], lr, li)
    x_last = x_inj[rows - 1:rows, :]
    carry_ref[:, :HALF_STATE] = nr + x_last[:, :HALF_STATE]
    carry_ref[:, HALF_STATE:] = ni + x_last[:, HALF_STATE:]

    y = (y_intra + jnp.dot(hs_ref[...].astype(BF16), q_ref[0], preferred_element_type=F32)
         + d_ref[0] * v32)
    for t in range(SSM_T):
        y_ref[0, pl.ds(t, rows, stride=SSM_T), :] = y[:, t * LANES:(t + 1) * LANES]


def _ssm_scan(u_slabs, m_op, p_op, q_op, a_tab, d_tab, tb_rows):
    s = u_slabs.shape[1]
    rows = tb_rows // SSM_T
    op_spec = pl.BlockSpec((1, SLAB_K, SLAB_K), lambda i, j: (i, 0, 0))
    return pl.pallas_call(
        functools.partial(_ssm_scan_kernel, rows=rows),
        out_shape=jax.ShapeDtypeStruct((N_SLABS, s, LANES), F32),
        grid=(N_SLABS, s // tb_rows),
        in_specs=[pl.BlockSpec((1, tb_rows, LANES), lambda i, j: (i, j, 0)),
                  op_spec, op_spec, op_spec,
                  pl.BlockSpec((1, SSM_T, SLAB_STATE), lambda i, j: (i, 0, 0)),
                  pl.BlockSpec((1, 1, SLAB_K), lambda i, j: (i, 0, 0))],
        out_specs=pl.BlockSpec((1, tb_rows, LANES), lambda i, j: (i, j, 0)),
        scratch_shapes=[pltpu.VMEM((rows, SLAB_STATE), F32),
                        pltpu.VMEM((SSM_T, SLAB_STATE), F32)],
        name="ssm_scan", compiler_params=_cparams(("parallel", "arbitrary")),
    )(u_slabs, m_op, p_op, q_op, a_tab, d_tab)


def _gelu_tanh(x):
    c = math.sqrt(2.0 / math.pi)
    return 0.5 * x * (1.0 + jnp.tanh(c * (x + 0.044715 * (x * x * x))))


def _glu_kernel(y_ref, w_ref, g_ref, o_ref):
    y = jnp.concatenate([y_ref[k] for k in range(N_SLABS)], axis=-1)
    z = jnp.dot(_gelu_tanh(y).astype(BF16), w_ref[...], preferred_element_type=F32)
    out = z[:, :D_MODEL] * jax.nn.sigmoid(z[:, D_MODEL:]) * g_ref[...].astype(F32)
    o_ref[...] = out.astype(BF16)


def _glu(y_slabs, w_glu, gates, tm):
    s = y_slabs.shape[1]
    return pl.pallas_call(
        _glu_kernel,
        out_shape=jax.ShapeDtypeStruct((s, D_MODEL), BF16),
        grid=(s // tm,),
        in_specs=[pl.BlockSpec((N_SLABS, tm, LANES), lambda i: (0, i, 0)),
                  _resident(w_glu.shape, lambda i: (0, 0)),
                  pl.BlockSpec((tm, D_MODEL), lambda i: (i, 0))],
        out_specs=pl.BlockSpec((tm, D_MODEL), lambda i: (i, 0)),
        name="glu", compiler_params=_cparams(("parallel",)),
    )(y_slabs, w_glu, gates)


def _rms(x, g):
    return x * lax.rsqrt(jnp.mean(x * x, axis=-1, keepdims=True) + RMS_EPS) * g


_NT = (((1,), (1,)), ((), ()))


def _qkv_kernel(cq_ref, ckv_ref, kr_ref, pos_ref, invf_ref, gq_ref, gkv_ref,
                wqt_ref, wk_ref, wvt_ref, qt_ref, k_ref, vt_ref, *, scale):
    tm = cq_ref.shape[0]
    cqn = _rms(cq_ref[...], gq_ref[...]).astype(BF16)
    ckvn = _rms(ckv_ref[...], gkv_ref[...]).astype(BF16)

    ang = invf_ref[...] * pos_ref[...].astype(F32)
    cos_t = jnp.cos(ang)
    sin_t = jnp.sin(ang)

    qt = lax.dot_general(wqt_ref[...], cqn, _NT, preferred_element_type=F32) * scale
    rope_hi = QK_NOPE + QK_ROPE
    for h in range(N_HEADS):
        b = h * QK_PAD
        t1 = qt[b + QK_NOPE:b + QK_NOPE + HALF_ROPE]
        t2 = qt[b + QK_NOPE + HALF_ROPE:b + rope_hi]
        qt_ref[h, 0:QK_NOPE, :] = qt[b:b + QK_NOPE].astype(BF16)
        qt_ref[h, QK_NOPE:QK_NOPE + HALF_ROPE, :] = (t1 * cos_t - t2 * sin_t).astype(BF16)
        qt_ref[h, QK_NOPE + HALF_ROPE:rope_hi, :] = (t1 * sin_t + t2 * cos_t).astype(BF16)
        qt_ref[h, rope_hi:QK_PAD, :] = jnp.zeros((QK_PAD - rope_hi, tm), BF16)

    kr_t = kr_ref[...].T
    k1 = kr_t[0:HALF_ROPE]
    k2 = kr_t[HALF_ROPE:QK_ROPE]
    krot_t = jnp.concatenate(
        [k1 * cos_t - k2 * sin_t, k1 * sin_t + k2 * cos_t,
         jnp.zeros((LANES - QK_ROPE, tm), F32)], axis=0)
    krot = krot_t.T.astype(BF16)

    kn = jnp.dot(ckvn, wk_ref[...], preferred_element_type=F32)
    vt = lax.dot_general(wvt_ref[...], ckvn, _NT, preferred_element_type=F32)
    for h in range(N_HEADS):
        k_ref[h, :, 0:QK_NOPE] = kn[:, h * QK_NOPE:(h + 1) * QK_NOPE].astype(BF16)
        k_ref[h, :, QK_NOPE:QK_PAD] = krot
        vt_ref[h, 0:V_HEAD, :] = vt[h * V_HEAD:(h + 1) * V_HEAD].astype(BF16)
        vt_ref[h, V_HEAD:V_ROWS, :] = jnp.ones((V_ROWS - V_HEAD, tm), BF16)


def _qkv(c_lat, pos_row, inv_freq, gq, gkv, wqt, wk, wvt, tm):
    s = c_lat.shape[0]
    scale = math.log2(math.e) / math.sqrt(QK_NOPE + QK_ROPE)
    ncq = Q_LORA // Q_LORA
    return pl.pallas_call(
        functools.partial(_qkv_kernel, scale=scale),
        out_shape=(jax.ShapeDtypeStruct((N_HEADS, QK_PAD, s), BF16),
                   jax.ShapeDtypeStruct((N_HEADS, s, QK_PAD), BF16),
                   jax.ShapeDtypeStruct((N_HEADS, V_ROWS, s), BF16)),
        grid=(s // tm,),
        in_specs=[pl.BlockSpec((tm, Q_LORA), lambda i: (i, 0)),
                  pl.BlockSpec((tm, KV_LORA), lambda i: (i, ncq)),
                  pl.BlockSpec((tm, LANES), lambda i: (i, (Q_LORA + KV_LORA) // LANES)),
                  pl.BlockSpec((1, tm), lambda i: (0, i)),
                  _resident((HALF_ROPE, 1), lambda i: (0, 0)),
                  _resident((1, Q_LORA), lambda i: (0, 0)),
                  _resident((1, KV_LORA), lambda i: (0, 0)),
                  _resident(wqt.shape, lambda i: (0, 0)),
                  _resident(wk.shape, lambda i: (0, 0)),
                  _resident(wvt.shape, lambda i: (0, 0))],
        out_specs=(pl.BlockSpec((N_HEADS, QK_PAD, tm), lambda i: (0, 0, i)),
                   pl.BlockSpec((N_HEADS, tm, QK_PAD), lambda i: (0, i, 0)),
                   pl.BlockSpec((N_HEADS, V_ROWS, tm), lambda i: (0, 0, i))),
        name="qkv", compiler_params=_cparams(("parallel",)),
    )(c_lat, c_lat, c_lat, pos_row, inv_freq, gq, gkv, wqt, wk, wvt)


_NEG = -1e30


ATTN_UNROLL = 8


def _attn_kernel(qt_ref, k_ref, vt_ref, o_ref, s_ref, mx_ref, acc_ref, m_ref, *, t):
    seq = k_ref.shape[1]
    tq = 2 * t
    krow = lax.broadcasted_iota(jnp.int32, (t, t), 0)
    qcol = lax.broadcasted_iota(jnp.int32, (t, t), 1)
    causal = krow <= qcol

    def pv(ks, p):
        return jnp.dot(vt_ref[0, :, pl.ds(ks, t)], p, preferred_element_type=F32)

    def query_tile(i, carry):
        kd = pl.multiple_of(i * tq, tq)
        n_full = 2 * i

        def qk(ks, q0, nq):
            return jnp.dot(k_ref[0, pl.ds(ks, t), :], qt_ref[0, :, pl.ds(kd + q0, nq)],
                           preferred_element_type=F32)

        def put_scores(slot, ks):
            s = qk(ks, 0, tq)
            s_ref[slot] = s
            mx_ref[slot] = jnp.max(s, axis=0, keepdims=True)

        put_scores(0, 0)
        m_ref[...] = jnp.full_like(m_ref, _NEG)
        acc_ref[...] = jnp.zeros_like(acc_ref)

        def item(slot, j):
            put_scores(1 - slot, pl.multiple_of((j + 1) * t, t))
            m_old = m_ref[...]
            m_new = jnp.maximum(m_old, mx_ref[slot])
            p = jnp.exp2(s_ref[slot] - m_new).astype(BF16)
            acc_ref[...] = jnp.exp2(m_old - m_new) * acc_ref[...] + pv(pl.multiple_of(j * t, t), p)
            m_ref[...] = m_new

        def run(width, j0):
            for u in range(width):
                item(u % 2, j0 + u)

        done = 0
        width = 2
        while width < ATTN_UNROLL:
            bit = lax.rem(n_full // width, 2)
            pl.when(bit == 1)(functools.partial(run, width, done))
            done = done + bit * width
            width *= 2

        def main(r, c):
            run(ATTN_UNROLL, done + ATTN_UNROLL * r)
            return c

        lax.fori_loop(0, n_full // ATTN_UNROLL, main, 0)

        s1 = jnp.where(causal, qk(kd + t, t, t), _NEG)
        s0 = s_ref[0]
        s0 = jnp.concatenate([jnp.where(causal, s0[:, :t], _NEG), s0[:, t:]], axis=1)
        m_old = m_ref[...]
        m0 = jnp.maximum(m_old, jnp.max(s0, axis=0, keepdims=True))
        acc0 = jnp.exp2(m_old - m0) * acc_ref[...] + pv(kd, jnp.exp2(s0 - m0).astype(BF16))
        m0r = m0[:, t:]
        m1 = jnp.maximum(m0r, jnp.max(s1, axis=0, keepdims=True))
        acc1 = jnp.exp2(m0r - m1) * acc0[:, t:] + pv(kd + t, jnp.exp2(s1 - m1).astype(BF16))
        acc = jnp.concatenate([acc0[:, :t], acc1], axis=1)
        o_ref[pl.ds(kd, tq), :] = (acc[:V_HEAD] * (1.0 / acc[V_HEAD:V_HEAD + 1])).T.astype(BF16)
        return carry

    lax.fori_loop(0, seq // tq, query_tile, 0)


def _attention(qt, k, vt, t):
    s = k.shape[1]
    head = lambda h: (h, 0, 0)
    return pl.pallas_call(
        functools.partial(_attn_kernel, t=t),
        out_shape=jax.ShapeDtypeStruct((s, N_HEADS * V_HEAD), BF16),
        grid=(N_HEADS,),
        in_specs=[pl.BlockSpec((1, QK_PAD, s), head),
                  pl.BlockSpec((1, s, QK_PAD), head),
                  pl.BlockSpec((1, V_ROWS, s), head)],
        out_specs=pl.BlockSpec((s, V_HEAD), lambda h: (0, h), pipeline_mode=pl.Buffered(1)),
        scratch_shapes=[pltpu.VMEM((2, t, 2 * t), F32), pltpu.VMEM((2, 1, 2 * t), F32),
                        pltpu.VMEM((V_ROWS, 2 * t), F32), pltpu.VMEM((1, 2 * t), F32)],
        name="attn", compiler_params=_cparams(("parallel",)),
    )(qt, k, vt)


def _layer_norm(r, g, b):
    mu = jnp.mean(r, axis=-1, keepdims=True)
    c = r - mu
    var = jnp.mean(c * c, axis=-1, keepdims=True)
    return c * lax.rsqrt(var + LN_EPS) * g + b


def _outproj_kernel(sg_ref, gm_ref, mla_ref, x_ref, w_ref, g_ref, b_ref, h_ref, hb_ref):
    merged = sg_ref[...].astype(F32) + gm_ref[...].astype(F32) * mla_ref[...].astype(F32)
    mix = jnp.dot(merged.astype(BF16), w_ref[...], preferred_element_type=F32)
    h = _layer_norm(DEEPNORM_ALPHA * x_ref[...] + mix, g_ref[...], b_ref[...])
    h_ref[...] = h
    hb_ref[...] = h.astype(BF16)


def _outproj(ssm_gated, gates, mla, x2, w_out, ln_g, ln_b, tm):
    s = x2.shape[0]
    row = lambda i: (i, 0)
    return pl.pallas_call(
        _outproj_kernel,
        out_shape=(jax.ShapeDtypeStruct((s, D_MODEL), F32),
                   jax.ShapeDtypeStruct((s, D_MODEL), BF16)),
        grid=(s // tm,),
        in_specs=[pl.BlockSpec((tm, D_MODEL), row),
                  pl.BlockSpec((tm, D_MODEL), lambda i: (i, 1)),
                  pl.BlockSpec((tm, D_MODEL), row),
                  pl.BlockSpec((tm, D_MODEL), row),
                  _resident(w_out.shape, lambda i: (0, 0)),
                  _resident((1, D_MODEL), lambda i: (0, 0)),
                  _resident((1, D_MODEL), lambda i: (0, 0))],
        out_specs=(pl.BlockSpec((tm, D_MODEL), row), pl.BlockSpec((tm, D_MODEL), row)),
        name="outproj", compiler_params=_cparams(("parallel",)),
    )(ssm_gated, gates, mla, x2, w_out, ln_g, ln_b)


FFN_RESIDUAL_CHUNKS = 8


def _ffn_kernel(hb_ref, h_ref, wg_ref, wu_ref, wd_ref, g_ref, b_ref, o_ref):
    j = pl.program_id(1)
    chunk = h_ref.shape[0]

    @pl.when(j == 0)
    def _():
        o_ref[...] = jnp.zeros_like(o_ref)

    hb = hb_ref[...]
    gate = jnp.dot(hb, wg_ref[...], preferred_element_type=F32)
    up = jnp.dot(hb, wu_ref[...], preferred_element_type=F32)
    act = (gate * jax.nn.sigmoid(gate) * up).astype(BF16)
    o_ref[...] += jnp.dot(act, wd_ref[...], preferred_element_type=F32)

    @pl.when(j < FFN_RESIDUAL_CHUNKS)
    def _():
        rows = pl.ds(pl.multiple_of(j * chunk, chunk), chunk)
        o_ref[rows, :] += DEEPNORM_ALPHA * h_ref[...]

    @pl.when(j == pl.num_programs(1) - 1)
    def _():
        o_ref[...] = _layer_norm(o_ref[...], g_ref[...], b_ref[...])


def _ffn(hb, h1, wg, wu, wd, ln_g, ln_b, tm, tf):
    s = h1.shape[0]
    dff = wg.shape[1]
    steps = dff // tf
    assert steps >= FFN_RESIDUAL_CHUNKS and tm % FFN_RESIDUAL_CHUNKS == 0
    chunk = tm // FFN_RESIDUAL_CHUNKS
    return pl.pallas_call(
        _ffn_kernel,
        out_shape=jax.ShapeDtypeStruct((s, D_MODEL), F32),
        grid=(s // tm, steps),
        in_specs=[pl.BlockSpec((tm, D_MODEL), lambda i, j: (i, 0)),
                  pl.BlockSpec((chunk, D_MODEL),
                               lambda i, j: (i * FFN_RESIDUAL_CHUNKS
                                             + jnp.minimum(j, FFN_RESIDUAL_CHUNKS - 1), 0)),
                  pl.BlockSpec((D_MODEL, tf), lambda i, j: (0, j)),
                  pl.BlockSpec((D_MODEL, tf), lambda i, j: (0, j)),
                  pl.BlockSpec((tf, D_MODEL), lambda i, j: (j, 0)),
                  _resident((1, D_MODEL), lambda i, j: (0, 0)),
                  _resident((1, D_MODEL), lambda i, j: (0, 0))],
        out_specs=pl.BlockSpec((tm, D_MODEL), lambda i, j: (i, 0)),
        name="ffn", compiler_params=_cparams(("parallel", "arbitrary")),
    )(hb, h1, wg, wu, wd, ln_g, ln_b)


def _tile(s, want):
    t = min(s, want)
    assert s % t == 0, (s, t)
    return t


def kernel(x, positions, w_in, ssm_lambda_re, ssm_lambda_im, ssm_log_dt, ssm_b_re, ssm_b_im,
           ssm_c_re, ssm_c_im, ssm_d, w_glu, q_norm_g, w_uq, kv_norm_g, w_ukv, w_out,
           ln1_g, ln1_b, w_ffn_gate, w_ffn_up, w_ffn_down, ln2_g, ln2_b):
    bsz, seq, d_model = x.shape
    assert bsz == 1 and d_model == D_MODEL and w_in.shape[0] == DEPTH
    x2 = x.reshape(seq, D_MODEL)
    pos_row = positions.reshape(1, seq)
    inv_freq = (1.0 / (ROPE_THETA ** (jnp.arange(0, QK_ROPE, 2, dtype=F32) / QK_ROPE))
                ).reshape(HALF_ROPE, 1)
    h = x2
    for l in range(DEPTH):
        lat_hi = SSM_WIDTH + Q_LORA + KV_LORA + QK_ROPE
        w_a = jnp.pad(w_in[l][:, :lat_hi], ((0, 0), (0, LANES - QK_ROPE))).astype(BF16)
        w_g = w_in[l][:, lat_hi:].astype(BF16)
        wq = jnp.pad(w_uq[l].reshape(Q_LORA, N_HEADS, QK_NOPE + QK_ROPE),
                     ((0, 0), (0, 0), (0, QK_PAD - QK_NOPE - QK_ROPE)))
        wqt = wq.reshape(Q_LORA, N_HEADS * QK_PAD).T.astype(BF16)
        wkv = w_ukv[l].reshape(KV_LORA, N_HEADS, QK_NOPE + V_HEAD)
        wk = wkv[:, :, :QK_NOPE].reshape(KV_LORA, N_HEADS * QK_NOPE).astype(BF16)
        wvt = wkv[:, :, QK_NOPE:].reshape(KV_LORA, N_HEADS * V_HEAD).T.astype(BF16)

        u_slabs, c_lat = _inproj_a(h, w_a, _tile(seq, ROWS_INPROJ))
        gates = _inproj_g(h, w_g, _tile(seq, ROWS_INPROJ))

        m_op, p_op, q_op, a_tab = _ssm_prep(ssm_lambda_re[l], ssm_lambda_im[l], ssm_log_dt[l],
                                            ssm_b_re[l], ssm_b_im[l], ssm_c_re[l], ssm_c_im[l])
        d_tab = jnp.tile(ssm_d[l].reshape(N_SLABS, 1, LANES), (1, 1, SSM_T))
        y_slabs = _ssm_scan(u_slabs, m_op, p_op, q_op, a_tab, d_tab, _tile(seq, ROWS_SSM))
        ssm_gated = _glu(y_slabs, w_glu[l].astype(BF16), gates, _tile(seq, ROWS_GLU))

        qt, k, vt = _qkv(c_lat, pos_row, inv_freq, q_norm_g[l].reshape(1, Q_LORA),
                         kv_norm_g[l].reshape(1, KV_LORA), wqt, wk, wvt, _tile(seq, ROWS_QKV))
        mla = _attention(qt, k, vt, _tile(seq // 2, ROWS_ATTN))

        h, hb = _outproj(ssm_gated, gates, mla, h, w_out[l].astype(BF16),
                         ln1_g[l].reshape(1, D_MODEL), ln1_b[l].reshape(1, D_MODEL),
                         _tile(seq, ROWS_OUTPROJ))

        h = _ffn(hb, h, w_ffn_gate[l].astype(BF16), w_ffn_up[l].astype(BF16),
                 w_ffn_down[l].astype(BF16), ln2_g[l].reshape(1, D_MODEL),
                 ln2_b[l].reshape(1, D_MODEL), _tile(seq, ROWS_FFN), COLS_FFN)
    return h.reshape(bsz, seq, D_MODEL)
```

```python
import functools
import math

import jax
import jax.numpy as jnp
from jax import lax
from jax.experimental import pallas as pl
from jax.experimental.pallas import tpu as pltpu

F32 = jnp.float32
BF16 = jnp.bfloat16

D_MODEL = 2048
SSM_GROUP = 16
SSM_WIDTH = D_MODEL // 2
SSM_GROUPS = SSM_WIDTH // SSM_GROUP
SSM_STATE = 64
N_HEADS = 16
QK_NOPE = 128
QK_ROPE = 64
V_HEAD = 128
Q_LORA = 512
KV_LORA = 512
ROPE_THETA = 10000.0
DEPTH = 1
DEEPNORM_ALPHA = (2.0 * DEPTH) ** 0.25
LN_EPS = 1e-5
RMS_EPS = 1e-6

LANES = 128
V7X_VMEM_BYTES = 64 * 1024 * 1024
VMEM_LIMIT = 56 * 1024 * 1024

HALF_ROPE = QK_ROPE // 2
QK_PAD = 256
SSM_T = 8
SLAB_GROUPS = LANES // SSM_GROUP
N_SLABS = SSM_WIDTH // LANES
SLAB_K = SSM_T * LANES
SLAB_STATE = SLAB_GROUPS * 2 * SSM_STATE
HALF_STATE = SLAB_STATE // 2
V_ROWS = V_HEAD + 16

ROWS_INPROJ = 512
ROWS_SSM = 4096
ROWS_GLU = 512
ROWS_QKV = 512
ROWS_ATTN = 512
ROWS_OUTPROJ = 512
ROWS_FFN = 1024
COLS_FFN = 512


def _cparams(sem, vmem=VMEM_LIMIT):
    return pltpu.CompilerParams(dimension_semantics=sem, vmem_limit_bytes=vmem)


def _resident(shape, index_map):
    return pl.BlockSpec(shape, index_map, pipeline_mode=pl.Buffered(1))


def _ssm_prep_kernel(lre_ref, lim_ref, ldt_ref, btr_ref, bti_ref, cr_ref, ci_ref,
                     lre_flat_ref, lim_flat_ref, ldt_flat_ref, m_ref, p_ref, q_ref, a_ref):
    gl, n, pp, t_len = SLAB_GROUPS, SSM_STATE, SSM_GROUP, SSM_T
    lre = lre_ref[...]
    lim = lim_ref[...]
    dt = jnp.exp(ldt_ref[...])

    def apow(xr, xi, k):
        mag = jnp.exp(xr * float(k))
        ang = xi * float(k)
        return mag * jnp.cos(ang), mag * jnp.sin(ang)

    xr = lre * dt
    xi = lim * dt
    ar, ai = apow(xr, xi, 1)
    den = lre * lre + lim * lim
    nr = ar - 1.0
    coef_re = (nr * lre + ai * lim) / den
    coef_im = (ai * lre - nr * lim) / den
    btr = btr_ref[...]
    bti = bti_ref[...]
    bbr = coef_re * btr - coef_im * bti
    bbi = coef_re * bti + coef_im * btr
    cr = cr_ref[...]
    ci = ci_ref[...]

    def spread(width, period):
        r = lax.broadcasted_iota(jnp.int32, (period, width), 0)
        c = lax.broadcasted_iota(jnp.int32, (period, width), 1)
        return (c % period == r).astype(BF16)

    def same_group(rows, row_period, cols, col_period):
        r = lax.broadcasted_iota(jnp.int32, (rows, cols), 0)
        c = lax.broadcasted_iota(jnp.int32, (rows, cols), 1)
        return r // row_period == c // col_period

    rep_p = spread(LANES, pp)
    rep_n = spread(HALF_STATE, n)
    mask_pp = same_group(LANES, pp, LANES, pp)
    mask_pn = same_group(LANES, pp, HALF_STATE, n)

    def block_diag(x, rep, mask):
        x2 = x.reshape(gl * pp, x.shape[-1]).astype(BF16)
        return jnp.where(mask, jnp.dot(x2, rep, preferred_element_type=F32), 0.0)

    m_ref[...] = jnp.zeros_like(m_ref)
    dn = (((2,), (2,)), ((0,), (0,)))
    for k in range(t_len + 1):
        pr, pi = (jnp.ones_like(xr), jnp.zeros_like(xr)) if k == 0 else apow(xr, xi, k)
        car = cr * pr - ci * pi
        cai = cr * pi + ci * pr
        if k < t_len:
            resp = (lax.dot_general(bbr, car, dn, precision=lax.Precision.HIGHEST,
                                    preferred_element_type=F32)
                    - lax.dot_general(bbi, cai, dn, precision=lax.Precision.HIGHEST,
                                      preferred_element_type=F32))
            tile = block_diag(resp, rep_p, mask_pp).astype(BF16)
            for t0 in range(t_len - k):
                m_ref[0, t0 * LANES:(t0 + 1) * LANES, (t0 + k) * LANES:(t0 + k + 1) * LANES] = tile
            t0 = t_len - 1 - k
            for ri, inj in enumerate((pr * bbr - pi * bbi, pr * bbi + pi * bbr)):
                p_ref[0, t0 * LANES:(t0 + 1) * LANES, ri * HALF_STATE:(ri + 1) * HALF_STATE] = (
                    block_diag(inj, rep_n, mask_pn).astype(BF16))
        if k >= 1:
            for ri, ca in enumerate((car, -cai)):
                z = block_diag(ca, rep_n, mask_pn)
                q_ref[0, ri * HALF_STATE:(ri + 1) * HALF_STATE, (k - 1) * LANES:k * LANES] = (
                    z.T.astype(BF16))

    dtf = jnp.exp(ldt_flat_ref[0])
    xrf = lre_flat_ref[0] * dtf
    xif = lim_flat_ref[0] * dtf
    for j in range(1, t_len + 1):
        pr, pi = apow(xrf, xif, t_len * j)
        a_ref[0, j - 1:j, 0:HALF_STATE] = pr
        a_ref[0, j - 1:j, HALF_STATE:] = pi


def _ssm_prep(lam_re, lam_im, log_dt, b_re, b_im, c_re, c_im):
    g, n, p, s = SSM_GROUPS, SSM_STATE, SSM_GROUP, N_SLABS
    gl = SLAB_GROUPS
    grp = lambda shape: pl.BlockSpec((gl,) + shape, lambda i: (i, 0, 0))
    flat = pl.BlockSpec((1, 1, HALF_STATE), lambda i: (i, 0, 0))
    op = pl.BlockSpec((1, SLAB_K, SLAB_K), lambda i: (i, 0, 0))
    return pl.pallas_call(
        _ssm_prep_kernel,
        out_shape=(jax.ShapeDtypeStruct((s, SLAB_K, SLAB_K), BF16),
                   jax.ShapeDtypeStruct((s, SLAB_K, SLAB_STATE), BF16),
                   jax.ShapeDtypeStruct((s, SLAB_STATE, SLAB_K), BF16),
                   jax.ShapeDtypeStruct((s, SSM_T, SLAB_STATE), F32)),
        grid=(s,),
        in_specs=[grp((1, n)), grp((1, n)), grp((1, 1)), grp((p, n)), grp((p, n)),
                  grp((p, n)), grp((p, n)), flat, flat, flat],
        out_specs=(op, op, op, pl.BlockSpec((1, SSM_T, SLAB_STATE), lambda i: (i, 0, 0))),
        name="ssm_prep", compiler_params=_cparams(("parallel",)),
    )(lam_re.reshape(g, 1, n), lam_im.reshape(g, 1, n), log_dt.reshape(g, 1, 1),
      jnp.swapaxes(b_re, 1, 2), jnp.swapaxes(b_im, 1, 2), c_re, c_im,
      lam_re.reshape(s, 1, HALF_STATE), lam_im.reshape(s, 1, HALF_STATE),
      jnp.repeat(log_dt, n).reshape(s, 1, HALF_STATE))


def _inproj_a_kernel(x_ref, w_ref, u_ref, c_ref):
    z = jnp.dot(x_ref[...].astype(BF16), w_ref[...], preferred_element_type=F32)
    for k in range(N_SLABS):
        u_ref[k] = z[:, k * LANES:(k + 1) * LANES]
    c_ref[...] = z[:, SSM_WIDTH:]


def _inproj_a(x2, w_a, tm):
    s = x2.shape[0]
    nc = w_a.shape[1] - SSM_WIDTH
    return pl.pallas_call(
        _inproj_a_kernel,
        out_shape=(jax.ShapeDtypeStruct((N_SLABS, s, LANES), F32),
                   jax.ShapeDtypeStruct((s, nc), F32)),
        grid=(s // tm,),
        in_specs=[pl.BlockSpec((tm, D_MODEL), lambda i: (i, 0)),
                  _resident(w_a.shape, lambda i: (0, 0))],
        out_specs=(pl.BlockSpec((N_SLABS, tm, LANES), lambda i: (0, i, 0)),
                   pl.BlockSpec((tm, nc), lambda i: (i, 0))),
        name="inproj_a", compiler_params=_cparams(("parallel",)),
    )(x2, w_a)


def _inproj_g_kernel(x_ref, w_ref, g_ref):
    z = jnp.dot(x_ref[...].astype(BF16), w_ref[...], preferred_element_type=F32)
    g_ref[...] = jax.nn.sigmoid(z).astype(BF16)


def _inproj_g(x2, w_g, tm):
    s = x2.shape[0]
    ng = w_g.shape[1]
    return pl.pallas_call(
        _inproj_g_kernel,
        out_shape=jax.ShapeDtypeStruct((s, ng), BF16),
        grid=(s // tm,),
        in_specs=[pl.BlockSpec((tm, D_MODEL), lambda i: (i, 0)),
                  _resident(w_g.shape, lambda i: (0, 0))],
        out_specs=pl.BlockSpec((tm, ng), lambda i: (i, 0)),
        name="inproj_g", compiler_params=_cparams(("parallel",)),
    )(x2, w_g)


def _cmul(ar, ai, br, bi):
    return ar * br - ai * bi, ar * bi + ai * br


def _ssm_scan_kernel(u_ref, m_ref, p_ref, q_ref, a_ref, d_ref, y_ref, hs_ref, carry_ref, *, rows):
    tb = pl.program_id(1)

    @pl.when(tb == 0)
    def _():
        carry_ref[...] = jnp.zeros_like(carry_ref)

    v32 = jnp.concatenate(
        [u_ref[0, pl.ds(t, rows, stride=SSM_T), :] for t in range(SSM_T)], axis=-1)
    vb = v32.astype(BF16)
    x_inj = jnp.dot(vb, p_ref[0], preferred_element_type=F32)
    y_intra = jnp.dot(vb, m_ref[0], preferred_element_type=F32)

    a_tab = a_ref[0]
    row = lax.broadcasted_iota(jnp.int32, (rows, HALF_STATE), 0)
    xs = pltpu.roll(x_inj, 1, 0)
    first = row == 0
    tiles = rows // SSM_T
    re = jnp.where(first, carry_ref[:, :HALF_STATE][0:1], xs[:, :HALF_STATE]).reshape(
        tiles, SSM_T, HALF_STATE)
    im = jnp.where(first, carry_ref[:, HALF_STATE:][0:1], xs[:, HALF_STATE:]).reshape(
        tiles, SSM_T, HALF_STATE)
    sub = lax.broadcasted_iota(jnp.int32, (SSM_T, HALF_STATE), 0)
    for d in (1, 2, 4):
        ar = jnp.where(sub >= d, a_tab[d - 1:d, :HALF_STATE], 0.0)
        ai = jnp.where(sub >= d, a_tab[d - 1:d, HALF_STATE:], 0.0)
        pr, pi = _cmul(ar, ai, pltpu.roll(re, d, 1), pltpu.roll(im, d, 1))
        re = re + pr
        im = im + pi
    hs_ref[:, :HALF_STATE] = re.reshape(rows, HALF_STATE)
    hs_ref[:, HALF_STATE:] = im.reshape(rows, HALF_STATE)

    tab_r = a_tab[:, :HALF_STATE]
    tab_i = a_tab[:, HALF_STATE:]

    def tile_body(k, last):
        lr, li = last
        r0 = pl.multiple_of(k * SSM_T, SSM_T)
        cr, ci = _cmul(tab_r, tab_i, lr, li)
        hr = hs_ref[pl.ds(r0, SSM_T), :HALF_STATE] + cr
        hi = hs_ref[pl.ds(r0, SSM_T), HALF_STATE:] + ci
        hs_ref[pl.ds(r0, SSM_T), :HALF_STATE] = hr
        hs_ref[pl.ds(r0, SSM_T), HALF_STATE:] = hi
        return (jnp.broadcast_to(hr[SSM_T - 1:SSM_T], (SSM_T, HALF_STATE)),
                jnp.broadcast_to(hi[SSM_T - 1:SSM_T], (SSM_T, HALF_STATE)))

    zero = jnp.zeros((SSM_T, HALF_STATE), F32)
    lr, li = lax.fori_loop(0, rows // SSM_T, tile_body, (zero, zero))

    nr, ni = _cmul(a_tab[0:1, :HALF_STATE], a_tab[0:1, HALF_STATE:], lr, li)
    x_last = x_inj[rows - 1:rows, :]
    carry_ref[:, :HALF_STATE] = nr + x_last[:, :HALF_STATE]
    carry_ref[:, HALF_STATE:] = ni + x_last[:, HALF_STATE:]

    y = (y_intra + jnp.dot(hs_ref[...].astype(BF16), q_ref[0], preferred_element_type=F32)
         + d_ref[0] * v32)
    for t in range(SSM_T):
        y_ref[0, pl.ds(t, rows, stride=SSM_T), :] = y[:, t * LANES:(t + 1) * LANES]


def _ssm_scan(u_slabs, m_op, p_op, q_op, a_tab, d_tab, tb_rows):
    s = u_slabs.shape[1]
    rows = tb_rows // SSM_T
    op_spec = pl.BlockSpec((1, SLAB_K, SLAB_K), lambda i, j: (i, 0, 0))
    return pl.pallas_call(
        functools.partial(_ssm_scan_kernel, rows=rows),
        out_shape=jax.ShapeDtypeStruct((N_SLABS, s, LANES), F32),
        grid=(N_SLABS, s // tb_rows),
        in_specs=[pl.BlockSpec((1, tb_rows, LANES), lambda i, j: (i, j, 0)),
                  op_spec, op_spec, op_spec,
                  pl.BlockSpec((1, SSM_T, SLAB_STATE), lambda i, j: (i, 0, 0)),
                  pl.BlockSpec((1, 1, SLAB_K), lambda i, j: (i, 0, 0))],
        out_specs=pl.BlockSpec((1, tb_rows, LANES), lambda i, j: (i, j, 0)),
        scratch_shapes=[pltpu.VMEM((rows, SLAB_STATE), F32),
                        pltpu.VMEM((SSM_T, SLAB_STATE), F32)],
        name="ssm_scan", compiler_params=_cparams(("parallel", "arbitrary")),
    )(u_slabs, m_op, p_op, q_op, a_tab, d_tab)


def _gelu_tanh(x):
    c = math.sqrt(2.0 / math.pi)
    return 0.5 * x * (1.0 + jnp.tanh(c * (x + 0.044715 * (x * x * x))))


def _glu_kernel(y_ref, w_ref, g_ref, o_ref):
    y = jnp.concatenate([y_ref[k] for k in range(N_SLABS)], axis=-1)
    z = jnp.dot(_gelu_tanh(y).astype(BF16), w_ref[...], preferred_element_type=F32)
    out = z[:, :D_MODEL] * jax.nn.sigmoid(z[:, D_MODEL:]) * g_ref[...].astype(F32)
    o_ref[...] = out.astype(BF16)


def _glu(y_slabs, w_glu, gates, tm):
    s = y_slabs.shape[1]
    return pl.pallas_call(
        _glu_kernel,
        out_shape=jax.ShapeDtypeStruct((s, D_MODEL), BF16),
        grid=(s // tm,),
        in_specs=[pl.BlockSpec((N_SLABS, tm, LANES), lambda i: (0, i, 0)),
                  _resident(w_glu.shape, lambda i: (0, 0)),
                  pl.BlockSpec((tm, D_MODEL), lambda i: (i, 0))],
        out_specs=pl.BlockSpec((tm, D_MODEL), lambda i: (i, 0)),
        name="glu", compiler_params=_cparams(("parallel",)),
    )(y_slabs, w_glu, gates)


def _rms(x, g):
    return x * lax.rsqrt(jnp.mean(x * x, axis=-1, keepdims=True) + RMS_EPS) * g


_NT = (((1,), (1,)), ((), ()))


def _qkv_kernel(cq_ref, ckv_ref, kr_ref, pos_ref, invf_ref, gq_ref, gkv_ref,
                wqt_ref, wk_ref, wvt_ref, qt_ref, k_ref, vt_ref, *, scale):
    tm = cq_ref.shape[0]
    cqn = _rms(cq_ref[...], gq_ref[...]).astype(BF16)
    ckvn = _rms(ckv_ref[...], gkv_ref[...]).astype(BF16)

    ang = invf_ref[...] * pos_ref[...].astype(F32)
    cos_t = jnp.cos(ang)
    sin_t = jnp.sin(ang)

    qt = lax.dot_general(wqt_ref[...], cqn, _NT, preferred_element_type=F32) * scale
    rope_hi = QK_NOPE + QK_ROPE
    for h in range(N_HEADS):
        b = h * QK_PAD
        t1 = qt[b + QK_NOPE:b + QK_NOPE + HALF_ROPE]
        t2 = qt[b + QK_NOPE + HALF_ROPE:b + rope_hi]
        qt_ref[h, 0:QK_NOPE, :] = qt[b:b + QK_NOPE].astype(BF16)
        qt_ref[h, QK_NOPE:QK_NOPE + HALF_ROPE, :] = (t1 * cos_t - t2 * sin_t).astype(BF16)
        qt_ref[h, QK_NOPE + HALF_ROPE:rope_hi, :] = (t1 * sin_t + t2 * cos_t).astype(BF16)
        qt_ref[h, rope_hi:QK_PAD, :] = jnp.zeros((QK_PAD - rope_hi, tm), BF16)

    kr_t = kr_ref[...].T
    k1 = kr_t[0:HALF_ROPE]
    k2 = kr_t[HALF_ROPE:QK_ROPE]
    krot_t = jnp.concatenate(
        [k1 * cos_t - k2 * sin_t, k1 * sin_t + k2 * cos_t,
         jnp.zeros((LANES - QK_ROPE, tm), F32)], axis=0)
    krot = krot_t.T.astype(BF16)

    kn = jnp.dot(ckvn, wk_ref[...], preferred_element_type=F32)
    vt = lax.dot_general(wvt_ref[...], ckvn, _NT, preferred_element_type=F32)
    for h in range(N_HEADS):
        k_ref[h, :, 0:QK_NOPE] = kn[:, h * QK_NOPE:(h + 1) * QK_NOPE].astype(BF16)
        k_ref[h, :, QK_NOPE:QK_PAD] = krot
        vt_ref[h, 0:V_HEAD, :] = vt[h * V_HEAD:(h + 1) * V_HEAD].astype(BF16)
        vt_ref[h, V_HEAD:V_ROWS, :] = jnp.ones((V_ROWS - V_HEAD, tm), BF16)


def _qkv(c_lat, pos_row, inv_freq, gq, gkv, wqt, wk, wvt, tm):
    s = c_lat.shape[0]
    scale = math.log2(math.e) / math.sqrt(QK_NOPE + QK_ROPE)
    ncq = Q_LORA // Q_LORA
    return pl.pallas_call(
        functools.partial(_qkv_kernel, scale=scale),
        out_shape=(jax.ShapeDtypeStruct((N_HEADS, QK_PAD, s), BF16),
                   jax.ShapeDtypeStruct((N_HEADS, s, QK_PAD), BF16),
                   jax.ShapeDtypeStruct((N_HEADS, V_ROWS, s), BF16)),
        grid=(s // tm,),
        in_specs=[pl.BlockSpec((tm, Q_LORA), lambda i: (i, 0)),
                  pl.BlockSpec((tm, KV_LORA), lambda i: (i, ncq)),
                  pl.BlockSpec((tm, LANES), lambda i: (i, (Q_LORA + KV_LORA) // LANES)),
                  pl.BlockSpec((1, tm), lambda i: (0, i)),
                  _resident((HALF_ROPE, 1), lambda i: (0, 0)),
                  _resident((1, Q_LORA), lambda i: (0, 0)),
                  _resident((1, KV_LORA), lambda i: (0, 0)),
                  _resident(wqt.shape, lambda i: (0, 0)),
                  _resident(wk.shape, lambda i: (0, 0)),
                  _resident(wvt.shape, lambda i: (0, 0))],
        out_specs=(pl.BlockSpec((N_HEADS, QK_PAD, tm), lambda i: (0, 0, i)),
                   pl.BlockSpec((N_HEADS, tm, QK_PAD), lambda i: (0, i, 0)),
                   pl.BlockSpec((N_HEADS, V_ROWS, tm), lambda i: (0, 0, i))),
        name="qkv", compiler_params=_cparams(("parallel",)),
    )(c_lat, c_lat, c_lat, pos_row, inv_freq, gq, gkv, wqt, wk, wvt)


_NEG = -1e30


ATTN_UNROLL = 8


def _attn_kernel(qt_ref, k_ref, vt_ref, o_ref, s_ref, mx_ref, acc_ref, m_ref, *, t):
    seq = k_ref.shape[1]
    tq = 2 * t
    krow = lax.broadcasted_iota(jnp.int32, (t, t), 0)
    qcol = lax.broadcasted_iota(jnp.int32, (t, t), 1)
    causal = krow <= qcol

    def pv(ks, p):
        return jnp.dot(vt_ref[0, :, pl.ds(ks, t)], p, preferred_element_type=F32)

    def put_scores(slot, ks, q_start):
        s = jnp.dot(k_ref[0, pl.ds(ks, t), :], qt_ref[0, :, pl.ds(q_start, tq)],
                    preferred_element_type=F32)
        s_ref[slot] = s
        mx_ref[slot] = jnp.max(s, axis=0, keepdims=True)

    def query_tile(i, carry):
        kd = pl.multiple_of(i * tq, tq)
        n_full = 2 * i

        def qk(ks, q0, nq):
            return jnp.dot(k_ref[0, pl.ds(ks, t), :], qt_ref[0, :, pl.ds(kd + q0, nq)],
                           preferred_element_type=F32)

        m_ref[...] = jnp.full_like(m_ref, _NEG)
        acc_ref[...] = jnp.zeros_like(acc_ref)

        def item(slot, j):
            put_scores(1 - slot, pl.multiple_of((j + 1) * t, t), kd)
            m_old = m_ref[...]
            m_new = jnp.maximum(m_old, mx_ref[slot])
            p = jnp.exp2(s_ref[slot] - m_new).astype(BF16)
            acc_ref[...] = jnp.exp2(m_old - m_new) * acc_ref[...] + pv(pl.multiple_of(j * t, t), p)
            m_ref[...] = m_new

        def run(width, j0):
            for u in range(width):
                item(u % 2, j0 + u)

        done = 0
        width = 2
        while width < ATTN_UNROLL:
            bit = lax.rem(n_full // width, 2)
            pl.when(bit == 1)(functools.partial(run, width, done))
            done = done + bit * width
            width *= 2

        def main(r, c):
            run(ATTN_UNROLL, done + ATTN_UNROLL * r)
            return c

        lax.fori_loop(0, n_full // ATTN_UNROLL, main, 0)

        s1 = jnp.where(causal, qk(kd + t, t, t), _NEG)
        s0 = s_ref[0]
        s0 = jnp.concatenate([jnp.where(causal, s0[:, :t], _NEG), s0[:, t:]], axis=1)
        put_scores(0, 0, pl.multiple_of(jnp.minimum(kd + tq, seq - tq), tq))
        m_old = m_ref[...]
        m0 = jnp.maximum(m_old, jnp.max(s0, axis=0, keepdims=True))
        acc0 = jnp.exp2(m_old - m0) * acc_ref[...] + pv(kd, jnp.exp2(s0 - m0).astype(BF16))
        m0r = m0[:, t:]
        m1 = jnp.maximum(m0r, jnp.max(s1, axis=0, keepdims=True))
        acc1 = jnp.exp2(m0r - m1) * acc0[:, t:] + pv(kd + t, jnp.exp2(s1 - m1).astype(BF16))
        acc = jnp.concatenate([acc0[:, :t], acc1], axis=1)
        o_ref[pl.ds(kd, tq), :] = (acc[:V_HEAD] * (1.0 / acc[V_HEAD:V_HEAD + 1])).T.astype(BF16)
        return carry

    put_scores(0, 0, 0)
    lax.fori_loop(0, seq // tq, query_tile, 0)


def _attention(qt, k, vt, t):
    s = k.shape[1]
    head = lambda h: (h, 0, 0)
    return pl.pallas_call(
        functools.partial(_attn_kernel, t=t),
        out_shape=jax.ShapeDtypeStruct((s, N_HEADS * V_HEAD), BF16),
        grid=(N_HEADS,),
        in_specs=[pl.BlockSpec((1, QK_PAD, s), head),
                  pl.BlockSpec((1, s, QK_PAD), head),
                  pl.BlockSpec((1, V_ROWS, s), head)],
        out_specs=pl.BlockSpec((s, V_HEAD), lambda h: (0, h), pipeline_mode=pl.Buffered(1)),
        scratch_shapes=[pltpu.VMEM((2, t, 2 * t), F32), pltpu.VMEM((2, 1, 2 * t), F32),
                        pltpu.VMEM((V_ROWS, 2 * t), F32), pltpu.VMEM((1, 2 * t), F32)],
        name="attn", compiler_params=_cparams(("parallel",)),
    )(qt, k, vt)


def _layer_norm(r, g, b):
    mu = jnp.mean(r, axis=-1, keepdims=True)
    c = r - mu
    var = jnp.mean(c * c, axis=-1, keepdims=True)
    return c * lax.rsqrt(var + LN_EPS) * g + b


def _outproj_kernel(sg_ref, gm_ref, mla_ref, x_ref, w_ref, g_ref, b_ref, h_ref, hb_ref):
    merged = sg_ref[...].astype(F32) + gm_ref[...].astype(F32) * mla_ref[...].astype(F32)
    mix = jnp.dot(merged.astype(BF16), w_ref[...], preferred_element_type=F32)
    h = _layer_norm(DEEPNORM_ALPHA * x_ref[...] + mix, g_ref[...], b_ref[...])
    h_ref[...] = h
    hb_ref[...] = h.astype(BF16)


def _outproj(ssm_gated, gates, mla, x2, w_out, ln_g, ln_b, tm):
    s = x2.shape[0]
    row = lambda i: (i, 0)
    return pl.pallas_call(
        _outproj_kernel,
        out_shape=(jax.ShapeDtypeStruct((s, D_MODEL), F32),
                   jax.ShapeDtypeStruct((s, D_MODEL), BF16)),
        grid=(s // tm,),
        in_specs=[pl.BlockSpec((tm, D_MODEL), row),
                  pl.BlockSpec((tm, D_MODEL), lambda i: (i, 1)),
                  pl.BlockSpec((tm, D_MODEL), row),
                  pl.BlockSpec((tm, D_MODEL), row),
                  _resident(w_out.shape, lambda i: (0, 0)),
                  _resident((1, D_MODEL), lambda i: (0, 0)),
                  _resident((1, D_MODEL), lambda i: (0, 0))],
        out_specs=(pl.BlockSpec((tm, D_MODEL), row), pl.BlockSpec((tm, D_MODEL), row)),
        name="outproj", compiler_params=_cparams(("parallel",)),
    )(ssm_gated, gates, mla, x2, w_out, ln_g, ln_b)


FFN_RESIDUAL_CHUNKS = 8


def _ffn_kernel(hb_ref, h_ref, wg_ref, wu_ref, wd_ref, g_ref, b_ref, o_ref):
    j = pl.program_id(1)
    chunk = h_ref.shape[0]

    @pl.when(j == 0)
    def _():
        o_ref[...] = jnp.zeros_like(o_ref)

    hb = hb_ref[...]
    gate = jnp.dot(hb, wg_ref[...], preferred_element_type=F32)
    up = jnp.dot(hb, wu_ref[...], preferred_element_type=F32)
    act = (gate * jax.nn.sigmoid(gate) * up).astype(BF16)
    o_ref[...] += jnp.dot(act, wd_ref[...], preferred_element_type=F32)

    @pl.when(j < FFN_RESIDUAL_CHUNKS)
    def _():
        rows = pl.ds(pl.multiple_of(j * chunk, chunk), chunk)
        o_ref[rows, :] += DEEPNORM_ALPHA * h_ref[...]

    @pl.when(j == pl.num_programs(1) - 1)
    def _():
        o_ref[...] = _layer_norm(o_ref[...], g_ref[...], b_ref[...])


def _ffn(hb, h1, wg, wu, wd, ln_g, ln_b, tm, tf):
    s = h1.shape[0]
    dff = wg.shape[1]
    steps = dff // tf
    assert steps >= FFN_RESIDUAL_CHUNKS and tm % FFN_RESIDUAL_CHUNKS == 0
    chunk = tm // FFN_RESIDUAL_CHUNKS
    return pl.pallas_call(
        _ffn_kernel,
        out_shape=jax.ShapeDtypeStruct((s, D_MODEL), F32),
        grid=(s // tm, steps),
        in_specs=[pl.BlockSpec((tm, D_MODEL), lambda i, j: (i, 0)),
                  pl.BlockSpec((chunk, D_MODEL),
                               lambda i, j: (i * FFN_RESIDUAL_CHUNKS
                                             + jnp.minimum(j, FFN_RESIDUAL_CHUNKS - 1), 0)),
                  pl.BlockSpec((D_MODEL, tf), lambda i, j: (0, j)),
                  pl.BlockSpec((D_MODEL, tf), lambda i, j: (0, j)),
                  pl.BlockSpec((tf, D_MODEL), lambda i, j: (j, 0)),
                  _resident((1, D_MODEL), lambda i, j: (0, 0)),
                  _resident((1, D_MODEL), lambda i, j: (0, 0))],
        out_specs=pl.BlockSpec((tm, D_MODEL), lambda i, j: (i, 0)),
        name="ffn", compiler_params=_cparams(("parallel", "arbitrary")),
    )(hb, h1, wg, wu, wd, ln_g, ln_b)


def _tile(s, want):
    t = min(s, want)
    assert s % t == 0, (s, t)
    return t


def kernel(x, positions, w_in, ssm_lambda_re, ssm_lambda_im, ssm_log_dt, ssm_b_re, ssm_b_im,
           ssm_c_re, ssm_c_im, ssm_d, w_glu, q_norm_g, w_uq, kv_norm_g, w_ukv, w_out,
           ln1_g, ln1_b, w_ffn_gate, w_ffn_up, w_ffn_down, ln2_g, ln2_b):
    bsz, seq, d_model = x.shape
    assert bsz == 1 and d_model == D_MODEL and w_in.shape[0] == DEPTH
    x2 = x.reshape(seq, D_MODEL)
    pos_row = positions.reshape(1, seq)
    inv_freq = (1.0 / (ROPE_THETA ** (jnp.arange(0, QK_ROPE, 2, dtype=F32) / QK_ROPE))
                ).reshape(HALF_ROPE, 1)
    h = x2
    for l in range(DEPTH):
        lat_hi = SSM_WIDTH + Q_LORA + KV_LORA + QK_ROPE
        w_a = jnp.pad(w_in[l][:, :lat_hi], ((0, 0), (0, LANES - QK_ROPE))).astype(BF16)
        w_g = w_in[l][:, lat_hi:].astype(BF16)
        wq = jnp.pad(w_uq[l].reshape(Q_LORA, N_HEADS, QK_NOPE + QK_ROPE),
                     ((0, 0), (0, 0), (0, QK_PAD - QK_NOPE - QK_ROPE)))
        wqt = wq.reshape(Q_LORA, N_HEADS * QK_PAD).T.astype(BF16)
        wkv = w_ukv[l].reshape(KV_LORA, N_HEADS, QK_NOPE + V_HEAD)
        wk = wkv[:, :, :QK_NOPE].reshape(KV_LORA, N_HEADS * QK_NOPE).astype(BF16)
        wvt = wkv[:, :, QK_NOPE:].reshape(KV_LORA, N_HEADS * V_HEAD).T.astype(BF16)

        u_slabs, c_lat = _inproj_a(h, w_a, _tile(seq, ROWS_INPROJ))
        gates = _inproj_g(h, w_g, _tile(seq, ROWS_INPROJ))

        m_op, p_op, q_op, a_tab = _ssm_prep(ssm_lambda_re[l], ssm_lambda_im[l], ssm_log_dt[l],
                                            ssm_b_re[l], ssm_b_im[l], ssm_c_re[l], ssm_c_im[l])
        d_tab = jnp.tile(ssm_d[l].reshape(N_SLABS, 1, LANES), (1, 1, SSM_T))
        y_slabs = _ssm_scan(u_slabs, m_op, p_op, q_op, a_tab, d_tab, _tile(seq, ROWS_SSM))
        ssm_gated = _glu(y_slabs, w_glu[l].astype(BF16), gates, _tile(seq, ROWS_GLU))

        qt, k, vt = _qkv(c_lat, pos_row, inv_freq, q_norm_g[l].reshape(1, Q_LORA),
                         kv_norm_g[l].reshape(1, KV_LORA), wqt, wk, wvt, _tile(seq, ROWS_QKV))
        mla = _attention(qt, k, vt, _tile(seq // 2, ROWS_ATTN))

        h, hb = _outproj(ssm_gated, gates, mla, h, w_out[l].astype(BF16),
                         ln1_g[l].reshape(1, D_MODEL), ln1_b[l].reshape(1, D_MODEL),
                         _tile(seq, ROWS_OUTPROJ))

        h = _ffn(hb, h, w_ffn_gate[l].astype(BF16), w_ffn_up[l].astype(BF16),
                 w_ffn_down[l].astype(BF16), ln2_g[l].reshape(1, D_MODEL),
                 ln2_b[l].reshape(1, D_MODEL), _tile(seq, ROWS_FFN), COLS_FFN)
    return h.reshape(bsz, seq, D_MODEL)
```

```python
import functools
import math

import jax
import jax.numpy as jnp
from jax import lax
from jax.experimental import pallas as pl
from jax.experimental.pallas import tpu as pltpu

F32 = jnp.float32
BF16 = jnp.bfloat16

D_MODEL = 2048
SSM_GROUP = 16
SSM_WIDTH = D_MODEL // 2
SSM_GROUPS = SSM_WIDTH // SSM_GROUP
SSM_STATE = 64
N_HEADS = 16
QK_NOPE = 128
QK_ROPE = 64
V_HEAD = 128
Q_LORA = 512
KV_LORA = 512
ROPE_THETA = 10000.0
DEPTH = 1
DEEPNORM_ALPHA = (2.0 * DEPTH) ** 0.25
LN_EPS = 1e-5
RMS_EPS = 1e-6

LANES = 128
V7X_VMEM_BYTES = 64 * 1024 * 1024
VMEM_LIMIT = 56 * 1024 * 1024

HALF_ROPE = QK_ROPE // 2
QK_PAD = 256
SSM_T = 8
SLAB_GROUPS = LANES // SSM_GROUP
N_SLABS = SSM_WIDTH // LANES
SLAB_K = SSM_T * LANES
SLAB_STATE = SLAB_GROUPS * 2 * SSM_STATE
HALF_STATE = SLAB_STATE // 2
V_ROWS = V_HEAD + 16

ROWS_INPROJ = 512
ROWS_SSM = 4096
ROWS_GLU = 512
ROWS_QKV = 512
ROWS_ATTN = 512
ROWS_OUTPROJ = 512
ROWS_FFN = 1024
COLS_FFN = 512


def _cparams(sem, vmem=VMEM_LIMIT):
    return pltpu.CompilerParams(dimension_semantics=sem, vmem_limit_bytes=vmem)


def _resident(shape, index_map):
    return pl.BlockSpec(shape, index_map, pipeline_mode=pl.Buffered(1))


def _ssm_prep_kernel(lre_ref, lim_ref, ldt_ref, btr_ref, bti_ref, cr_ref, ci_ref,
                     lre_flat_ref, lim_flat_ref, ldt_flat_ref, m_ref, p_ref, q_ref, a_ref):
    gl, n, pp, t_len = SLAB_GROUPS, SSM_STATE, SSM_GROUP, SSM_T
    lre = lre_ref[...]
    lim = lim_ref[...]
    dt = jnp.exp(ldt_ref[...])

    def apow(xr, xi, k):
        mag = jnp.exp(xr * float(k))
        ang = xi * float(k)
        return mag * jnp.cos(ang), mag * jnp.sin(ang)

    xr = lre * dt
    xi = lim * dt
    ar, ai = apow(xr, xi, 1)
    den = lre * lre + lim * lim
    nr = ar - 1.0
    coef_re = (nr * lre + ai * lim) / den
    coef_im = (ai * lre - nr * lim) / den
    btr = btr_ref[...]
    bti = bti_ref[...]
    bbr = coef_re * btr - coef_im * bti
    bbi = coef_re * bti + coef_im * btr
    cr = cr_ref[...]
    ci = ci_ref[...]

    def spread(width, period):
        r = lax.broadcasted_iota(jnp.int32, (period, width), 0)
        c = lax.broadcasted_iota(jnp.int32, (period, width), 1)
        return (c % period == r).astype(BF16)

    def same_group(rows, row_period, cols, col_period):
        r = lax.broadcasted_iota(jnp.int32, (rows, cols), 0)
        c = lax.broadcasted_iota(jnp.int32, (rows, cols), 1)
        return r // row_period == c // col_period

    rep_p = spread(LANES, pp)
    rep_n = spread(HALF_STATE, n)
    mask_pp = same_group(LANES, pp, LANES, pp)
    mask_pn = same_group(LANES, pp, HALF_STATE, n)

    def block_diag(x, rep, mask):
        x2 = x.reshape(gl * pp, x.shape[-1]).astype(BF16)
        return jnp.where(mask, jnp.dot(x2, rep, preferred_element_type=F32), 0.0)

    m_ref[...] = jnp.zeros_like(m_ref)
    dn = (((2,), (2,)), ((0,), (0,)))
    for k in range(t_len + 1):
        pr, pi = (jnp.ones_like(xr), jnp.zeros_like(xr)) if k == 0 else apow(xr, xi, k)
        car = cr * pr - ci * pi
        cai = cr * pi + ci * pr
        if k < t_len:
            resp = (lax.dot_general(bbr, car, dn, precision=lax.Precision.HIGHEST,
                                    preferred_element_type=F32)
                    - lax.dot_general(bbi, cai, dn, precision=lax.Precision.HIGHEST,
                                      preferred_element_type=F32))
            tile = block_diag(resp, rep_p, mask_pp).astype(BF16)
            for t0 in range(t_len - k):
                m_ref[0, t0 * LANES:(t0 + 1) * LANES, (t0 + k) * LANES:(t0 + k + 1) * LANES] = tile
            t0 = t_len - 1 - k
            for ri, inj in enumerate((pr * bbr - pi * bbi, pr * bbi + pi * bbr)):
                p_ref[0, t0 * LANES:(t0 + 1) * LANES, ri * HALF_STATE:(ri + 1) * HALF_STATE] = (
                    block_diag(inj, rep_n, mask_pn).astype(BF16))
        if k >= 1:
            for ri, ca in enumerate((car, -cai)):
                z = block_diag(ca, rep_n, mask_pn)
                q_ref[0, ri * HALF_STATE:(ri + 1) * HALF_STATE, (k - 1) * LANES:k * LANES] = (
                    z.T.astype(BF16))

    dtf = jnp.exp(ldt_flat_ref[0])
    xrf = lre_flat_ref[0] * dtf
    xif = lim_flat_ref[0] * dtf
    for j in range(1, t_len + 1):
        pr, pi = apow(xrf, xif, t_len * j)
        a_ref[0, j - 1:j, 0:HALF_STATE] = pr
        a_ref[0, j - 1:j, HALF_STATE:] = pi


def _ssm_prep(lam_re, lam_im, log_dt, b_re, b_im, c_re, c_im):
    g, n, p, s = SSM_GROUPS, SSM_STATE, SSM_GROUP, N_SLABS
    gl = SLAB_GROUPS
    grp = lambda shape: pl.BlockSpec((gl,) + shape, lambda i: (i, 0, 0))
    flat = pl.BlockSpec((1, 1, HALF_STATE), lambda i: (i, 0, 0))
    op = pl.BlockSpec((1, SLAB_K, SLAB_K), lambda i: (i, 0, 0))
    return pl.pallas_call(
        _ssm_prep_kernel,
        out_shape=(jax.ShapeDtypeStruct((s, SLAB_K, SLAB_K), BF16),
                   jax.ShapeDtypeStruct((s, SLAB_K, SLAB_STATE), BF16),
                   jax.ShapeDtypeStruct((s, SLAB_STATE, SLAB_K), BF16),
                   jax.ShapeDtypeStruct((s, SSM_T, SLAB_STATE), F32)),
        grid=(s,),
        in_specs=[grp((1, n)), grp((1, n)), grp((1, 1)), grp((p, n)), grp((p, n)),
                  grp((p, n)), grp((p, n)), flat, flat, flat],
        out_specs=(op, op, op, pl.BlockSpec((1, SSM_T, SLAB_STATE), lambda i: (i, 0, 0))),
        name="ssm_prep", compiler_params=_cparams(("parallel",)),
    )(lam_re.reshape(g, 1, n), lam_im.reshape(g, 1, n), log_dt.reshape(g, 1, 1),
      jnp.swapaxes(b_re, 1, 2), jnp.swapaxes(b_im, 1, 2), c_re, c_im,
      lam_re.reshape(s, 1, HALF_STATE), lam_im.reshape(s, 1, HALF_STATE),
      jnp.repeat(log_dt, n).reshape(s, 1, HALF_STATE))


def _inproj_a_kernel(x_ref, w_ref, u_ref, c_ref):
    z = jnp.dot(x_ref[...].astype(BF16), w_ref[...], preferred_element_type=F32)
    for k in range(N_SLABS):
        u_ref[k] = z[:, k * LANES:(k + 1) * LANES]
    c_ref[...] = z[:, SSM_WIDTH:]


def _inproj_a(x2, w_a, tm):
    s = x2.shape[0]
    nc = w_a.shape[1] - SSM_WIDTH
    return pl.pallas_call(
        _inproj_a_kernel,
        out_shape=(jax.ShapeDtypeStruct((N_SLABS, s, LANES), F32),
                   jax.ShapeDtypeStruct((s, nc), F32)),
        grid=(s // tm,),
        in_specs=[pl.BlockSpec((tm, D_MODEL), lambda i: (i, 0)),
                  _resident(w_a.shape, lambda i: (0, 0))],
        out_specs=(pl.BlockSpec((N_SLABS, tm, LANES), lambda i: (0, i, 0)),
                   pl.BlockSpec((tm, nc), lambda i: (i, 0))),
        name="inproj_a", compiler_params=_cparams(("parallel",)),
    )(x2, w_a)


def _inproj_g_kernel(x_ref, w_ref, g_ref):
    z = jnp.dot(x_ref[...].astype(BF16), w_ref[...], preferred_element_type=F32)
    g_ref[...] = jax.nn.sigmoid(z).astype(BF16)


def _inproj_g(x2, w_g, tm):
    s = x2.shape[0]
    ng = w_g.shape[1]
    return pl.pallas_call(
        _inproj_g_kernel,
        out_shape=jax.ShapeDtypeStruct((s, ng), BF16),
        grid=(s // tm,),
        in_specs=[pl.BlockSpec((tm, D_MODEL), lambda i: (i, 0)),
                  _resident(w_g.shape, lambda i: (0, 0))],
        out_specs=pl.BlockSpec((tm, ng), lambda i: (i, 0)),
        name="inproj_g", compiler_params=_cparams(("parallel",)),
    )(x2, w_g)


def _cmul(ar, ai, br, bi):
    return ar * br - ai * bi, ar * bi + ai * br


def _ssm_scan_kernel(u_ref, m_ref, p_ref, q_ref, a_ref, d_ref, y_ref, hs_ref, carry_ref, *, rows):
    tb = pl.program_id(1)

    @pl.when(tb == 0)
    def _():
        carry_ref[...] = jnp.zeros_like(carry_ref)

    v32 = jnp.concatenate(
        [u_ref[0, pl.ds(t, rows, stride=SSM_T), :] for t in range(SSM_T)], axis=-1)
    vb = v32.astype(BF16)
    x_inj = jnp.dot(vb, p_ref[0], preferred_element_type=F32)
    y_intra = jnp.dot(vb, m_ref[0], preferred_element_type=F32)

    a_tab = a_ref[0]
    row = lax.broadcasted_iota(jnp.int32, (rows, HALF_STATE), 0)
    xs = pltpu.roll(x_inj, 1, 0)
    first = row == 0
    tiles = rows // SSM_T
    re = jnp.where(first, carry_ref[:, :HALF_STATE][0:1], xs[:, :HALF_STATE]).reshape(
        tiles, SSM_T, HALF_STATE)
    im = jnp.where(first, carry_ref[:, HALF_STATE:][0:1], xs[:, HALF_STATE:]).reshape(
        tiles, SSM_T, HALF_STATE)
    sub = lax.broadcasted_iota(jnp.int32, (SSM_T, HALF_STATE), 0)
    for d in (1, 2, 4):
        ar = jnp.where(sub >= d, a_tab[d - 1:d, :HALF_STATE], 0.0)
        ai = jnp.where(sub >= d, a_tab[d - 1:d, HALF_STATE:], 0.0)
        pr, pi = _cmul(ar, ai, pltpu.roll(re, d, 1), pltpu.roll(im, d, 1))
        re = re + pr
        im = im + pi
    hs_ref[:, :HALF_STATE] = re.reshape(rows, HALF_STATE)
    hs_ref[:, HALF_STATE:] = im.reshape(rows, HALF_STATE)

    tab_r = a_tab[:, :HALF_STATE]
    tab_i = a_tab[:, HALF_STATE:]

    def tile_body(k, last):
        lr, li = last
        r0 = pl.multiple_of(k * SSM_T, SSM_T)
        cr, ci = _cmul(tab_r, tab_i, lr, li)
        hr = hs_ref[pl.ds(r0, SSM_T), :HALF_STATE] + cr
        hi = hs_ref[pl.ds(r0, SSM_T), HALF_STATE:] + ci
        hs_ref[pl.ds(r0, SSM_T), :HALF_STATE] = hr
        hs_ref[pl.ds(r0, SSM_T), HALF_STATE:] = hi
        return (jnp.broadcast_to(hr[SSM_T - 1:SSM_T], (SSM_T, HALF_STATE)),
                jnp.broadcast_to(hi[SSM_T - 1:SSM_T], (SSM_T, HALF_STATE)))

    zero = jnp.zeros((SSM_T, HALF_STATE), F32)
    lr, li = lax.fori_loop(0, rows // SSM_T, tile_body, (zero, zero))

    nr, ni = _cmul(a_tab[0:1, :HALF_STATE], a_tab[0:1, HALF_STATE:], lr, li)
    x_last = x_inj[rows - 1:rows, :]
    carry_ref[:, :HALF_STATE] = nr + x_last[:, :HALF_STATE]
    carry_ref[:, HALF_STATE:] = ni + x_last[:, HALF_STATE:]

    y = (y_intra + jnp.dot(hs_ref[...].astype(BF16), q_ref[0], preferred_element_type=F32)
         + d_ref[0] * v32)
    for t in range(SSM_T):
        y_ref[0, pl.ds(t, rows, stride=SSM_T), :] = y[:, t * LANES:(t + 1) * LANES]


def _ssm_scan(u_slabs, m_op, p_op, q_op, a_tab, d_tab, tb_rows):
    s = u_slabs.shape[1]
    rows = tb_rows // SSM_T
    op_spec = pl.BlockSpec((1, SLAB_K, SLAB_K), lambda i, j: (i, 0, 0))
    return pl.pallas_call(
        functools.partial(_ssm_scan_kernel, rows=rows),
        out_shape=jax.ShapeDtypeStruct((N_SLABS, s, LANES), F32),
        grid=(N_SLABS, s // tb_rows),
        in_specs=[pl.BlockSpec((1, tb_rows, LANES), lambda i, j: (i, j, 0)),
                  op_spec, op_spec, op_spec,
                  pl.BlockSpec((1, SSM_T, SLAB_STATE), lambda i, j: (i, 0, 0)),
                  pl.BlockSpec((1, 1, SLAB_K), lambda i, j: (i, 0, 0))],
        out_specs=pl.BlockSpec((1, tb_rows, LANES), lambda i, j: (i, j, 0)),
        scratch_shapes=[pltpu.VMEM((rows, SLAB_STATE), F32),
                        pltpu.VMEM((SSM_T, SLAB_STATE), F32)],
        name="ssm_scan", compiler_params=_cparams(("parallel", "arbitrary")),
    )(u_slabs, m_op, p_op, q_op, a_tab, d_tab)


def _gelu_tanh(x):
    c = math.sqrt(2.0 / math.pi)
    return 0.5 * x * (1.0 + jnp.tanh(c * (x + 0.044715 * (x * x * x))))


def _glu_kernel(y_ref, w_ref, g_ref, o_ref):
    y = jnp.concatenate([y_ref[k] for k in range(N_SLABS)], axis=-1)
    z = jnp.dot(_gelu_tanh(y).astype(BF16), w_ref[...], preferred_element_type=F32)
    out = z[:, :D_MODEL] * jax.nn.sigmoid(z[:, D_MODEL:]) * g_ref[...].astype(F32)
    o_ref[...] = out.astype(BF16)


def _glu(y_slabs, w_glu, gates, tm):
    s = y_slabs.shape[1]
    return pl.pallas_call(
        _glu_kernel,
        out_shape=jax.ShapeDtypeStruct((s, D_MODEL), BF16),
        grid=(s // tm,),
        in_specs=[pl.BlockSpec((N_SLABS, tm, LANES), lambda i: (0, i, 0)),
                  _resident(w_glu.shape, lambda i: (0, 0)),
                  pl.BlockSpec((tm, D_MODEL), lambda i: (i, 0))],
        out_specs=pl.BlockSpec((tm, D_MODEL), lambda i: (i, 0)),
        name="glu", compiler_params=_cparams(("parallel",)),
    )(y_slabs, w_glu, gates)


def _rms(x, g):
    return x * lax.rsqrt(jnp.mean(x * x, axis=-1, keepdims=True) + RMS_EPS) * g


_NT = (((1,), (1,)), ((), ()))


def _qkv_kernel(cq_ref, ckv_ref, kr_ref, pos_ref, invf_ref, gq_ref, gkv_ref,
                wqt_ref, wk_ref, wvt_ref, qt_ref, k_ref, vt_ref, *, scale):
    tm = cq_ref.shape[0]
    cqn = _rms(cq_ref[...], gq_ref[...]).astype(BF16)
    ckvn = _rms(ckv_ref[...], gkv_ref[...]).astype(BF16)

    ang = invf_ref[...] * pos_ref[...].astype(F32)
    cos_t = jnp.cos(ang)
    sin_t = jnp.sin(ang)

    qt = lax.dot_general(wqt_ref[...], cqn, _NT, preferred_element_type=F32) * scale
    rope_hi = QK_NOPE + QK_ROPE
    for h in range(N_HEADS):
        b = h * QK_PAD
        t1 = qt[b + QK_NOPE:b + QK_NOPE + HALF_ROPE]
        t2 = qt[b + QK_NOPE + HALF_ROPE:b + rope_hi]
        qt_ref[h, 0:QK_NOPE, :] = qt[b:b + QK_NOPE].astype(BF16)
        qt_ref[h, QK_NOPE:QK_NOPE + HALF_ROPE, :] = (t1 * cos_t - t2 * sin_t).astype(BF16)
        qt_ref[h, QK_NOPE + HALF_ROPE:rope_hi, :] = (t1 * sin_t + t2 * cos_t).astype(BF16)
        qt_ref[h, rope_hi:QK_PAD, :] = jnp.zeros((QK_PAD - rope_hi, tm), BF16)

    kr_t = kr_ref[...].T
    k1 = kr_t[0:HALF_ROPE]
    k2 = kr_t[HALF_ROPE:QK_ROPE]
    krot_t = jnp.concatenate(
        [k1 * cos_t - k2 * sin_t, k1 * sin_t + k2 * cos_t,
         jnp.zeros((LANES - QK_ROPE, tm), F32)], axis=0)
    krot = krot_t.T.astype(BF16)

    kn = jnp.dot(ckvn, wk_ref[...], preferred_element_type=F32)
    vt = lax.dot_general(wvt_ref[...], ckvn, _NT, preferred_element_type=F32)
    for h in range(N_HEADS):
        k_ref[h, :, 0:QK_NOPE] = kn[:, h * QK_NOPE:(h + 1) * QK_NOPE].astype(BF16)
        k_ref[h, :, QK_NOPE:QK_PAD] = krot
        vt_ref[h, 0:V_HEAD, :] = vt[h * V_HEAD:(h + 1) * V_HEAD].astype(BF16)
        vt_ref[h, V_HEAD:V_ROWS, :] = jnp.ones((V_ROWS - V_HEAD, tm), BF16)


def _qkv(c_lat, pos_row, inv_freq, gq, gkv, wqt, wk, wvt, tm):
    s = c_lat.shape[0]
    scale = math.log2(math.e) / math.sqrt(QK_NOPE + QK_ROPE)
    ncq = Q_LORA // Q_LORA
    return pl.pallas_call(
        functools.partial(_qkv_kernel, scale=scale),
        out_shape=(jax.ShapeDtypeStruct((N_HEADS, QK_PAD, s), BF16),
                   jax.ShapeDtypeStruct((N_HEADS, s, QK_PAD), BF16),
                   jax.ShapeDtypeStruct((N_HEADS, V_ROWS, s), BF16)),
        grid=(s // tm,),
        in_specs=[pl.BlockSpec((tm, Q_LORA), lambda i: (i, 0)),
                  pl.BlockSpec((tm, KV_LORA), lambda i: (i, ncq)),
                  pl.BlockSpec((tm, LANES), lambda i: (i, (Q_LORA + KV_LORA) // LANES)),
                  pl.BlockSpec((1, tm), lambda i: (0, i)),
                  _resident((HALF_ROPE, 1), lambda i: (0, 0)),
                  _resident((1, Q_LORA), lambda i: (0, 0)),
                  _resident((1, KV_LORA), lambda i: (0, 0)),
                  _resident(wqt.shape, lambda i: (0, 0)),
                  _resident(wk.shape, lambda i: (0, 0)),
                  _resident(wvt.shape, lambda i: (0, 0))],
        out_specs=(pl.BlockSpec((N_HEADS, QK_PAD, tm), lambda i: (0, 0, i)),
                   pl.BlockSpec((N_HEADS, tm, QK_PAD), lambda i: (0, i, 0)),
                   pl.BlockSpec((N_HEADS, V_ROWS, tm), lambda i: (0, 0, i))),
        name="qkv", compiler_params=_cparams(("parallel",)),
    )(c_lat, c_lat, c_lat, pos_row, inv_freq, gq, gkv, wqt, wk, wvt)


_NEG = -1e30


ATTN_UNROLL = 8


def _attn_kernel(qt_ref, k_ref, vt_ref, o_ref, s_ref, mx_ref, acc_ref, m_ref, fin_ref, *, t):
    seq = k_ref.shape[1]
    tq = 2 * t
    krow = lax.broadcasted_iota(jnp.int32, (t, t), 0)
    qcol = lax.broadcasted_iota(jnp.int32, (t, t), 1)
    causal = krow <= qcol

    def pv(ks, p):
        return jnp.dot(vt_ref[0, :, pl.ds(ks, t)], p, preferred_element_type=F32)

    def put_scores(slot, ks, q_start):
        s = jnp.dot(k_ref[0, pl.ds(ks, t), :], qt_ref[0, :, pl.ds(q_start, tq)],
                    preferred_element_type=F32)
        s_ref[slot] = s
        mx_ref[slot] = jnp.max(s, axis=0, keepdims=True)

    def flush(q_start):
        acc = fin_ref[...]
        o_ref[pl.ds(q_start, tq), :] = (acc[:V_HEAD] * (1.0 / acc[V_HEAD:V_HEAD + 1])).T.astype(BF16)

    def query_tile(i, carry):
        kd = pl.multiple_of(i * tq, tq)
        n_full = 2 * i

        def qk(ks, q0, nq):
            return jnp.dot(k_ref[0, pl.ds(ks, t), :], qt_ref[0, :, pl.ds(kd + q0, nq)],
                           preferred_element_type=F32)

        m_ref[...] = jnp.full_like(m_ref, _NEG)
        acc_ref[...] = jnp.zeros_like(acc_ref)

        def item(slot, j):
            put_scores(1 - slot, pl.multiple_of((j + 1) * t, t), kd)
            m_old = m_ref[...]
            m_new = jnp.maximum(m_old, mx_ref[slot])
            p = jnp.exp2(s_ref[slot] - m_new).astype(BF16)
            acc_ref[...] = jnp.exp2(m_old - m_new) * acc_ref[...] + pv(pl.multiple_of(j * t, t), p)
            m_ref[...] = m_new

        def run(width, j0):
            for u in range(width):
                item(u % 2, j0 + u)

        done = 0
        width = 2
        while width < ATTN_UNROLL:
            bit = lax.rem(n_full // width, 2)
            pl.when(bit == 1)(functools.partial(run, width, done))
            done = done + bit * width
            width *= 2

        def main(r, c):
            run(ATTN_UNROLL, done + ATTN_UNROLL * r)
            return c

        lax.fori_loop(0, n_full // ATTN_UNROLL, main, 0)

        flush(pl.multiple_of(jnp.maximum(kd - tq, 0), tq))
        s1 = jnp.where(causal, qk(kd + t, t, t), _NEG)
        s0 = s_ref[0]
        s0 = jnp.concatenate([jnp.where(causal, s0[:, :t], _NEG), s0[:, t:]], axis=1)
        put_scores(0, 0, pl.multiple_of(jnp.minimum(kd + tq, seq - tq), tq))
        m_old = m_ref[...]
        m0 = jnp.maximum(m_old, jnp.max(s0, axis=0, keepdims=True))
        acc0 = jnp.exp2(m_old - m0) * acc_ref[...] + pv(kd, jnp.exp2(s0 - m0).astype(BF16))
        m0r = m0[:, t:]
        m1 = jnp.maximum(m0r, jnp.max(s1, axis=0, keepdims=True))
        acc1 = jnp.exp2(m0r - m1) * acc0[:, t:] + pv(kd + t, jnp.exp2(s1 - m1).astype(BF16))
        fin_ref[:, :t] = acc0[:, :t]
        fin_ref[:, t:] = acc1
        return carry

    put_scores(0, 0, 0)
    fin_ref[...] = jnp.ones_like(fin_ref)
    lax.fori_loop(0, seq // tq, query_tile, 0)
    flush(seq - tq)


def _attention(qt, k, vt, t):
    s = k.shape[1]
    head = lambda h: (h, 0, 0)
    return pl.pallas_call(
        functools.partial(_attn_kernel, t=t),
        out_shape=jax.ShapeDtypeStruct((s, N_HEADS * V_HEAD), BF16),
        grid=(N_HEADS,),
        in_specs=[pl.BlockSpec((1, QK_PAD, s), head),
                  pl.BlockSpec((1, s, QK_PAD), head),
                  pl.BlockSpec((1, V_ROWS, s), head)],
        out_specs=pl.BlockSpec((s, V_HEAD), lambda h: (0, h), pipeline_mode=pl.Buffered(1)),
        scratch_shapes=[pltpu.VMEM((2, t, 2 * t), F32), pltpu.VMEM((2, 1, 2 * t), F32),
                        pltpu.VMEM((V_ROWS, 2 * t), F32), pltpu.VMEM((1, 2 * t), F32),
                        pltpu.VMEM((V_ROWS, 2 * t), F32)],
        name="attn", compiler_params=_cparams(("parallel",)),
    )(qt, k, vt)


def _layer_norm(r, g, b):
    mu = jnp.mean(r, axis=-1, keepdims=True)
    c = r - mu
    var = jnp.mean(c * c, axis=-1, keepdims=True)
    return c * lax.rsqrt(var + LN_EPS) * g + b


def _outproj_kernel(sg_ref, gm_ref, mla_ref, x_ref, w_ref, g_ref, b_ref, h_ref, hb_ref):
    merged = sg_ref[...].astype(F32) + gm_ref[...].astype(F32) * mla_ref[...].astype(F32)
    mix = jnp.dot(merged.astype(BF16), w_ref[...], preferred_element_type=F32)
    h = _layer_norm(DEEPNORM_ALPHA * x_ref[...] + mix, g_ref[...], b_ref[...])
    h_ref[...] = h
    hb_ref[...] = h.astype(BF16)


def _outproj(ssm_gated, gates, mla, x2, w_out, ln_g, ln_b, tm):
    s = x2.shape[0]
    row = lambda i: (i, 0)
    return pl.pallas_call(
        _outproj_kernel,
        out_shape=(jax.ShapeDtypeStruct((s, D_MODEL), F32),
                   jax.ShapeDtypeStruct((s, D_MODEL), BF16)),
        grid=(s // tm,),
        in_specs=[pl.BlockSpec((tm, D_MODEL), row),
                  pl.BlockSpec((tm, D_MODEL), lambda i: (i, 1)),
                  pl.BlockSpec((tm, D_MODEL), row),
                  pl.BlockSpec((tm, D_MODEL), row),
                  _resident(w_out.shape, lambda i: (0, 0)),
                  _resident((1, D_MODEL), lambda i: (0, 0)),
                  _resident((1, D_MODEL), lambda i: (0, 0))],
        out_specs=(pl.BlockSpec((tm, D_MODEL), row), pl.BlockSpec((tm, D_MODEL), row)),
        name="outproj", compiler_params=_cparams(("parallel",)),
    )(ssm_gated, gates, mla, x2, w_out, ln_g, ln_b)


FFN_RESIDUAL_CHUNKS = 8


def _ffn_kernel(hb_ref, h_ref, wg_ref, wu_ref, wd_ref, g_ref, b_ref, o_ref):
    j = pl.program_id(1)
    chunk = h_ref.shape[0]

    @pl.when(j == 0)
    def _():
        o_ref[...] = jnp.zeros_like(o_ref)

    hb = hb_ref[...]
    gate = jnp.dot(hb, wg_ref[...], preferred_element_type=F32)
    up = jnp.dot(hb, wu_ref[...], preferred_element_type=F32)
    act = (gate * jax.nn.sigmoid(gate) * up).astype(BF16)
    o_ref[...] += jnp.dot(act, wd_ref[...], preferred_element_type=F32)

    @pl.when(j < FFN_RESIDUAL_CHUNKS)
    def _():
        rows = pl.ds(pl.multiple_of(j * chunk, chunk), chunk)
        o_ref[rows, :] += DEEPNORM_ALPHA * h_ref[...]

    @pl.when(j == pl.num_programs(1) - 1)
    def _():
        o_ref[...] = _layer_norm(o_ref[...], g_ref[...], b_ref[...])


def _ffn(hb, h1, wg, wu, wd, ln_g, ln_b, tm, tf):
    s = h1.shape[0]
    dff = wg.shape[1]
    steps = dff // tf
    assert steps >= FFN_RESIDUAL_CHUNKS and tm % FFN_RESIDUAL_CHUNKS == 0
    chunk = tm // FFN_RESIDUAL_CHUNKS
    return pl.pallas_call(
        _ffn_kernel,
        out_shape=jax.ShapeDtypeStruct((s, D_MODEL), F32),
        grid=(s // tm, steps),
        in_specs=[pl.BlockSpec((tm, D_MODEL), lambda i, j: (i, 0)),
                  pl.BlockSpec((chunk, D_MODEL),
                               lambda i, j: (i * FFN_RESIDUAL_CHUNKS
                                             + jnp.minimum(j, FFN_RESIDUAL_CHUNKS - 1), 0)),
                  pl.BlockSpec((D_MODEL, tf), lambda i, j: (0, j)),
                  pl.BlockSpec((D_MODEL, tf), lambda i, j: (0, j)),
                  pl.BlockSpec((tf, D_MODEL), lambda i, j: (j, 0)),
                  _resident((1, D_MODEL), lambda i, j: (0, 0)),
                  _resident((1, D_MODEL), lambda i, j: (0, 0))],
        out_specs=pl.BlockSpec((tm, D_MODEL), lambda i, j: (i, 0)),
        name="ffn", compiler_params=_cparams(("parallel", "arbitrary")),
    )(hb, h1, wg, wu, wd, ln_g, ln_b)


def _tile(s, want):
    t = min(s, want)
    assert s % t == 0, (s, t)
    return t


def kernel(x, positions, w_in, ssm_lambda_re, ssm_lambda_im, ssm_log_dt, ssm_b_re, ssm_b_im,
           ssm_c_re, ssm_c_im, ssm_d, w_glu, q_norm_g, w_uq, kv_norm_g, w_ukv, w_out,
           ln1_g, ln1_b, w_ffn_gate, w_ffn_up, w_ffn_down, ln2_g, ln2_b):
    bsz, seq, d_model = x.shape
    assert bsz == 1 and d_model == D_MODEL and w_in.shape[0] == DEPTH
    x2 = x.reshape(seq, D_MODEL)
    pos_row = positions.reshape(1, seq)
    inv_freq = (1.0 / (ROPE_THETA ** (jnp.arange(0, QK_ROPE, 2, dtype=F32) / QK_ROPE))
                ).reshape(HALF_ROPE, 1)
    h = x2
    for l in range(DEPTH):
        lat_hi = SSM_WIDTH + Q_LORA + KV_LORA + QK_ROPE
        w_a = jnp.pad(w_in[l][:, :lat_hi], ((0, 0), (0, LANES - QK_ROPE))).astype(BF16)
        w_g = w_in[l][:, lat_hi:].astype(BF16)
        wq = jnp.pad(w_uq[l].reshape(Q_LORA, N_HEADS, QK_NOPE + QK_ROPE),
                     ((0, 0), (0, 0), (0, QK_PAD - QK_NOPE - QK_ROPE)))
        wqt = wq.reshape(Q_LORA, N_HEADS * QK_PAD).T.astype(BF16)
        wkv = w_ukv[l].reshape(KV_LORA, N_HEADS, QK_NOPE + V_HEAD)
        wk = wkv[:, :, :QK_NOPE].reshape(KV_LORA, N_HEADS * QK_NOPE).astype(BF16)
        wvt = wkv[:, :, QK_NOPE:].reshape(KV_LORA, N_HEADS * V_HEAD).T.astype(BF16)

        u_slabs, c_lat = _inproj_a(h, w_a, _tile(seq, ROWS_INPROJ))
        gates = _inproj_g(h, w_g, _tile(seq, ROWS_INPROJ))

        m_op, p_op, q_op, a_tab = _ssm_prep(ssm_lambda_re[l], ssm_lambda_im[l], ssm_log_dt[l],
                                            ssm_b_re[l], ssm_b_im[l], ssm_c_re[l], ssm_c_im[l])
        d_tab = jnp.tile(ssm_d[l].reshape(N_SLABS, 1, LANES), (1, 1, SSM_T))
        y_slabs = _ssm_scan(u_slabs, m_op, p_op, q_op, a_tab, d_tab, _tile(seq, ROWS_SSM))
        ssm_gated = _glu(y_slabs, w_glu[l].astype(BF16), gates, _tile(seq, ROWS_GLU))

        qt, k, vt = _qkv(c_lat, pos_row, inv_freq, q_norm_g[l].reshape(1, Q_LORA),
                         kv_norm_g[l].reshape(1, KV_LORA), wqt, wk, wvt, _tile(seq, ROWS_QKV))
        mla = _attention(qt, k, vt, _tile(seq // 2, ROWS_ATTN))

        h, hb = _outproj(ssm_gated, gates, mla, h, w_out[l].astype(BF16),
                         ln1_g[l].reshape(1, D_MODEL), ln1_b[l].reshape(1, D_MODEL),
                         _tile(seq, ROWS_OUTPROJ))

        h = _ffn(hb, h, w_ffn_gate[l].astype(BF16), w_ffn_up[l].astype(BF16),
                 w_ffn_down[l].astype(BF16), ln2_g[l].reshape(1, D_MODEL),
                 ln2_b[l].reshape(1, D_MODEL), _tile(seq, ROWS_FFN), COLS_FFN)
    return h.reshape(bsz, seq, D_MODEL)
```

```python
import functools
import math

import jax
import jax.numpy as jnp
from jax import lax
from jax.experimental import pallas as pl
from jax.experimental.pallas import tpu as pltpu

F32 = jnp.float32
BF16 = jnp.bfloat16

D_MODEL = 2048
SSM_GROUP = 16
SSM_WIDTH = D_MODEL // 2
SSM_GROUPS = SSM_WIDTH // SSM_GROUP
SSM_STATE = 64
N_HEADS = 16
QK_NOPE = 128
QK_ROPE = 64
V_HEAD = 128
Q_LORA = 512
KV_LORA = 512
ROPE_THETA = 10000.0
DEPTH = 1
DEEPNORM_ALPHA = (2.0 * DEPTH) ** 0.25
LN_EPS = 1e-5
RMS_EPS = 1e-6

LANES = 128
BF16_SUBLANES = 16
V7X_MXU_DIM = 256
V7X_VMEM_BYTES = 64 * 1024 * 1024
VMEM_LIMIT = V7X_VMEM_BYTES - 8 * 1024 * 1024

HALF_ROPE = QK_ROPE // 2
QK_DIM = QK_NOPE + QK_ROPE
assert QK_DIM <= V7X_MXU_DIM and QK_DIM % BF16_SUBLANES == 0
SSM_T = 8
SLAB_GROUPS = LANES // SSM_GROUP
N_SLABS = SSM_WIDTH // LANES
SLAB_K = SSM_T * LANES
SLAB_STATE = SLAB_GROUPS * 2 * SSM_STATE
HALF_STATE = SLAB_STATE // 2
V_ROWS = V_HEAD + BF16_SUBLANES

ROWS_INPROJ = 512
ROWS_SSM = 4096
ROWS_GLU = 512
ROWS_QKV = 512
ROWS_ATTN = 512
ROWS_OUTPROJ = 512
ROWS_FFN = 1024
COLS_FFN = 512


def _cparams(sem, vmem=VMEM_LIMIT):
    return pltpu.CompilerParams(dimension_semantics=sem, vmem_limit_bytes=vmem)


def _resident(shape, index_map):
    return pl.BlockSpec(shape, index_map, pipeline_mode=pl.Buffered(1))


def _ssm_prep_kernel(lre_ref, lim_ref, ldt_ref, btr_ref, bti_ref, cr_ref, ci_ref,
                     lre_flat_ref, lim_flat_ref, ldt_flat_ref, m_ref, p_ref, q_ref, a_ref):
    gl, n, pp, t_len = SLAB_GROUPS, SSM_STATE, SSM_GROUP, SSM_T
    lre = lre_ref[...]
    lim = lim_ref[...]
    dt = jnp.exp(ldt_ref[...])

    def apow(xr, xi, k):
        mag = jnp.exp(xr * float(k))
        ang = xi * float(k)
        return mag * jnp.cos(ang), mag * jnp.sin(ang)

    xr = lre * dt
    xi = lim * dt
    ar, ai = apow(xr, xi, 1)
    den = lre * lre + lim * lim
    nr = ar - 1.0
    coef_re = (nr * lre + ai * lim) / den
    coef_im = (ai * lre - nr * lim) / den
    btr = btr_ref[...]
    bti = bti_ref[...]
    bbr = coef_re * btr - coef_im * bti
    bbi = coef_re * bti + coef_im * btr
    cr = cr_ref[...]
    ci = ci_ref[...]

    def spread(width, period):
        r = lax.broadcasted_iota(jnp.int32, (period, width), 0)
        c = lax.broadcasted_iota(jnp.int32, (period, width), 1)
        return (c % period == r).astype(BF16)

    def same_group(rows, row_period, cols, col_period):
        r = lax.broadcasted_iota(jnp.int32, (rows, cols), 0)
        c = lax.broadcasted_iota(jnp.int32, (rows, cols), 1)
        return r // row_period == c // col_period

    rep_p = spread(LANES, pp)
    rep_n = spread(HALF_STATE, n)
    mask_pp = same_group(LANES, pp, LANES, pp)
    mask_pn = same_group(LANES, pp, HALF_STATE, n)

    def block_diag(x, rep, mask):
        x2 = x.reshape(gl * pp, x.shape[-1]).astype(BF16)
        return jnp.where(mask, jnp.dot(x2, rep, preferred_element_type=F32), 0.0)

    m_ref[...] = jnp.zeros_like(m_ref)
    dn = (((2,), (2,)), ((0,), (0,)))
    for k in range(t_len + 1):
        pr, pi = (jnp.ones_like(xr), jnp.zeros_like(xr)) if k == 0 else apow(xr, xi, k)
        car = cr * pr - ci * pi
        cai = cr * pi + ci * pr
        if k < t_len:
            resp = (lax.dot_general(bbr, car, dn, precision=lax.Precision.HIGHEST,
                                    preferred_element_type=F32)
                    - lax.dot_general(bbi, cai, dn, precision=lax.Precision.HIGHEST,
                                      preferred_element_type=F32))
            tile = block_diag(resp, rep_p, mask_pp).astype(BF16)
            for t0 in range(t_len - k):
                m_ref[0, t0 * LANES:(t0 + 1) * LANES, (t0 + k) * LANES:(t0 + k + 1) * LANES] = tile
            t0 = t_len - 1 - k
            for ri, inj in enumerate((pr * bbr - pi * bbi, pr * bbi + pi * bbr)):
                p_ref[0, t0 * LANES:(t0 + 1) * LANES, ri * HALF_STATE:(ri + 1) * HALF_STATE] = (
                    block_diag(inj, rep_n, mask_pn).astype(BF16))
        if k >= 1:
            for ri, ca in enumerate((car, -cai)):
                z = block_diag(ca, rep_n, mask_pn)
                q_ref[0, ri * HALF_STATE:(ri + 1) * HALF_STATE, (k - 1) * LANES:k * LANES] = (
                    z.T.astype(BF16))

    dtf = jnp.exp(ldt_flat_ref[0])
    xrf = lre_flat_ref[0] * dtf
    xif = lim_flat_ref[0] * dtf
    for j in range(1, t_len + 1):
        pr, pi = apow(xrf, xif, t_len * j)
        a_ref[0, j - 1:j, 0:HALF_STATE] = pr
        a_ref[0, j - 1:j, HALF_STATE:] = pi


def _ssm_prep(lam_re, lam_im, log_dt, b_re, b_im, c_re, c_im):
    g, n, p, s = SSM_GROUPS, SSM_STATE, SSM_GROUP, N_SLABS
    gl = SLAB_GROUPS
    grp = lambda shape: pl.BlockSpec((gl,) + shape, lambda i: (i, 0, 0))
    flat = pl.BlockSpec((1, 1, HALF_STATE), lambda i: (i, 0, 0))
    op = pl.BlockSpec((1, SLAB_K, SLAB_K), lambda i: (i, 0, 0))
    return pl.pallas_call(
        _ssm_prep_kernel,
        out_shape=(jax.ShapeDtypeStruct((s, SLAB_K, SLAB_K), BF16),
                   jax.ShapeDtypeStruct((s, SLAB_K, SLAB_STATE), BF16),
                   jax.ShapeDtypeStruct((s, SLAB_STATE, SLAB_K), BF16),
                   jax.ShapeDtypeStruct((s, SSM_T, SLAB_STATE), F32)),
        grid=(s,),
        in_specs=[grp((1, n)), grp((1, n)), grp((1, 1)), grp((p, n)), grp((p, n)),
                  grp((p, n)), grp((p, n)), flat, flat, flat],
        out_specs=(op, op, op, pl.BlockSpec((1, SSM_T, SLAB_STATE), lambda i: (i, 0, 0))),
        name="ssm_prep", compiler_params=_cparams(("parallel",)),
    )(lam_re.reshape(g, 1, n), lam_im.reshape(g, 1, n), log_dt.reshape(g, 1, 1),
      jnp.swapaxes(b_re, 1, 2), jnp.swapaxes(b_im, 1, 2), c_re, c_im,
      lam_re.reshape(s, 1, HALF_STATE), lam_im.reshape(s, 1, HALF_STATE),
      jnp.repeat(log_dt, n).reshape(s, 1, HALF_STATE))


def _inproj_a_kernel(x_ref, w_ref, u_ref, c_ref):
    z = jnp.dot(x_ref[...].astype(BF16), w_ref[...], preferred_element_type=F32)
    for k in range(N_SLABS):
        u_ref[k] = z[:, k * LANES:(k + 1) * LANES]
    c_ref[...] = z[:, SSM_WIDTH:]


def _inproj_a(x2, w_a, tm):
    s = x2.shape[0]
    nc = w_a.shape[1] - SSM_WIDTH
    return pl.pallas_call(
        _inproj_a_kernel,
        out_shape=(jax.ShapeDtypeStruct((N_SLABS, s, LANES), F32),
                   jax.ShapeDtypeStruct((s, nc), F32)),
        grid=(s // tm,),
        in_specs=[pl.BlockSpec((tm, D_MODEL), lambda i: (i, 0)),
                  _resident(w_a.shape, lambda i: (0, 0))],
        out_specs=(pl.BlockSpec((N_SLABS, tm, LANES), lambda i: (0, i, 0)),
                   pl.BlockSpec((tm, nc), lambda i: (i, 0))),
        name="inproj_a", compiler_params=_cparams(("parallel",)),
    )(x2, w_a)


def _inproj_g_kernel(x_ref, w_ref, g_ref):
    z = jnp.dot(x_ref[...].astype(BF16), w_ref[...], preferred_element_type=F32)
    g_ref[...] = jax.nn.sigmoid(z).astype(BF16)


def _inproj_g(x2, w_g, tm):
    s = x2.shape[0]
    ng = w_g.shape[1]
    return pl.pallas_call(
        _inproj_g_kernel,
        out_shape=jax.ShapeDtypeStruct((s, ng), BF16),
        grid=(s // tm,),
        in_specs=[pl.BlockSpec((tm, D_MODEL), lambda i: (i, 0)),
                  _resident(w_g.shape, lambda i: (0, 0))],
        out_specs=pl.BlockSpec((tm, ng), lambda i: (i, 0)),
        name="inproj_g", compiler_params=_cparams(("parallel",)),
    )(x2, w_g)


def _cmul(ar, ai, br, bi):
    return ar * br - ai * bi, ar * bi + ai * br


def _ssm_scan_kernel(u_ref, m_ref, p_ref, q_ref, a_ref, d_ref, y_ref, hs_ref, carry_ref, *, rows):
    tb = pl.program_id(1)

    @pl.when(tb == 0)
    def _():
        carry_ref[...] = jnp.zeros_like(carry_ref)

    v32 = jnp.concatenate(
        [u_ref[0, pl.ds(t, rows, stride=SSM_T), :] for t in range(SSM_T)], axis=-1)
    vb = v32.astype(BF16)
    x_inj = jnp.dot(vb, p_ref[0], preferred_element_type=F32)
    y_intra = jnp.dot(vb, m_ref[0], preferred_element_type=F32)

    a_tab = a_ref[0]
    row = lax.broadcasted_iota(jnp.int32, (rows, HALF_STATE), 0)
    xs = pltpu.roll(x_inj, 1, 0)
    first = row == 0
    tiles = rows // SSM_T
    re = jnp.where(first, carry_ref[:, :HALF_STATE][0:1], xs[:, :HALF_STATE]).reshape(
        tiles, SSM_T, HALF_STATE)
    im = jnp.where(first, carry_ref[:, HALF_STATE:][0:1], xs[:, HALF_STATE:]).reshape(
        tiles, SSM_T, HALF_STATE)
    sub = lax.broadcasted_iota(jnp.int32, (SSM_T, HALF_STATE), 0)
    for d in (1, 2, 4):
        ar = jnp.where(sub >= d, a_tab[d - 1:d, :HALF_STATE], 0.0)
        ai = jnp.where(sub >= d, a_tab[d - 1:d, HALF_STATE:], 0.0)
        pr, pi = _cmul(ar, ai, pltpu.roll(re, d, 1), pltpu.roll(im, d, 1))
        re = re + pr
        im = im + pi
    hs_ref[:, :HALF_STATE] = re.reshape(rows, HALF_STATE)
    hs_ref[:, HALF_STATE:] = im.reshape(rows, HALF_STATE)

    tab_r = a_tab[:, :HALF_STATE]
    tab_i = a_tab[:, HALF_STATE:]

    def tile_body(k, last):
        lr, li = last
        r0 = pl.multiple_of(k * SSM_T, SSM_T)
        cr, ci = _cmul(tab_r, tab_i, lr, li)
        hr = hs_ref[pl.ds(r0, SSM_T), :HALF_STATE] + cr
        hi = hs_ref[pl.ds(r0, SSM_T), HALF_STATE:] + ci
        hs_ref[pl.ds(r0, SSM_T), :HALF_STATE] = hr
        hs_ref[pl.ds(r0, SSM_T), HALF_STATE:] = hi
        return (jnp.broadcast_to(hr[SSM_T - 1:SSM_T], (SSM_T, HALF_STATE)),
                jnp.broadcast_to(hi[SSM_T - 1:SSM_T], (SSM_T, HALF_STATE)))

    zero = jnp.zeros((SSM_T, HALF_STATE), F32)
    lr, li = lax.fori_loop(0, rows // SSM_T, tile_body, (zero, zero))

    nr, ni = _cmul(a_tab[0:1, :HALF_STATE], a_tab[0:1, HALF_STATE:], lr, li)
    x_last = x_inj[rows - 1:rows, :]
    carry_ref[:, :HALF_STATE] = nr + x_last[:, :HALF_STATE]
    carry_ref[:, HALF_STATE:] = ni + x_last[:, HALF_STATE:]

    y = (y_intra + jnp.dot(hs_ref[...].astype(BF16), q_ref[0], preferred_element_type=F32)
         + d_ref[0] * v32)
    for t in range(SSM_T):
        y_ref[0, pl.ds(t, rows, stride=SSM_T), :] = y[:, t * LANES:(t + 1) * LANES]


def _ssm_scan(u_slabs, m_op, p_op, q_op, a_tab, d_tab, tb_rows):
    s = u_slabs.shape[1]
    rows = tb_rows // SSM_T
    op_spec = pl.BlockSpec((1, SLAB_K, SLAB_K), lambda i, j: (i, 0, 0))
    return pl.pallas_call(
        functools.partial(_ssm_scan_kernel, rows=rows),
        out_shape=jax.ShapeDtypeStruct((N_SLABS, s, LANES), F32),
        grid=(N_SLABS, s // tb_rows),
        in_specs=[pl.BlockSpec((1, tb_rows, LANES), lambda i, j: (i, j, 0)),
                  op_spec, op_spec, op_spec,
                  pl.BlockSpec((1, SSM_T, SLAB_STATE), lambda i, j: (i, 0, 0)),
                  pl.BlockSpec((1, 1, SLAB_K), lambda i, j: (i, 0, 0))],
        out_specs=pl.BlockSpec((1, tb_rows, LANES), lambda i, j: (i, j, 0)),
        scratch_shapes=[pltpu.VMEM((rows, SLAB_STATE), F32),
                        pltpu.VMEM((SSM_T, SLAB_STATE), F32)],
        name="ssm_scan", compiler_params=_cparams(("parallel", "arbitrary")),
    )(u_slabs, m_op, p_op, q_op, a_tab, d_tab)


def _gelu_tanh(x):
    c = math.sqrt(2.0 / math.pi)
    return 0.5 * x * (1.0 + jnp.tanh(c * (x + 0.044715 * (x * x * x))))


def _glu_kernel(y_ref, w_ref, g_ref, o_ref):
    y = jnp.concatenate([y_ref[k] for k in range(N_SLABS)], axis=-1)
    z = jnp.dot(_gelu_tanh(y).astype(BF16), w_ref[...], preferred_element_type=F32)
    out = z[:, :D_MODEL] * jax.nn.sigmoid(z[:, D_MODEL:]) * g_ref[...].astype(F32)
    o_ref[...] = out.astype(BF16)


def _glu(y_slabs, w_glu, gates, tm):
    s = y_slabs.shape[1]
    return pl.pallas_call(
        _glu_kernel,
        out_shape=jax.ShapeDtypeStruct((s, D_MODEL), BF16),
        grid=(s // tm,),
        in_specs=[pl.BlockSpec((N_SLABS, tm, LANES), lambda i: (0, i, 0)),
                  _resident(w_glu.shape, lambda i: (0, 0)),
                  pl.BlockSpec((tm, D_MODEL), lambda i: (i, 0))],
        out_specs=pl.BlockSpec((tm, D_MODEL), lambda i: (i, 0)),
        name="glu", compiler_params=_cparams(("parallel",)),
    )(y_slabs, w_glu, gates)


def _rms(x, g):
    return x * lax.rsqrt(jnp.mean(x * x, axis=-1, keepdims=True) + RMS_EPS) * g


_NT = (((1,), (1,)), ((), ()))


def _qkv_kernel(cq_ref, ckv_ref, kr_ref, pos_ref, invf_ref, gq_ref, gkv_ref,
                wqt_ref, wk_ref, wvt_ref, qt_ref, k_ref, vt_ref, *, scale):
    tm = cq_ref.shape[0]
    cqn = _rms(cq_ref[...], gq_ref[...]).astype(BF16)
    ckvn = _rms(ckv_ref[...], gkv_ref[...]).astype(BF16)

    ang = invf_ref[...] * pos_ref[...].astype(F32)
    cos_t = jnp.cos(ang)
    sin_t = jnp.sin(ang)

    qt = lax.dot_general(wqt_ref[...], cqn, _NT, preferred_element_type=F32) * scale
    for h in range(N_HEADS):
        b = h * QK_DIM
        t1 = qt[b + QK_NOPE:b + QK_NOPE + HALF_ROPE]
        t2 = qt[b + QK_NOPE + HALF_ROPE:b + QK_DIM]
        qt_ref[h, 0:QK_NOPE, :] = qt[b:b + QK_NOPE].astype(BF16)
        qt_ref[h, QK_NOPE:QK_NOPE + HALF_ROPE, :] = (t1 * cos_t - t2 * sin_t).astype(BF16)
        qt_ref[h, QK_NOPE + HALF_ROPE:QK_DIM, :] = (t1 * sin_t + t2 * cos_t).astype(BF16)

    kr_t = kr_ref[...].T
    k1 = kr_t[0:HALF_ROPE]
    k2 = kr_t[HALF_ROPE:QK_ROPE]
    krot_t = jnp.concatenate(
        [k1 * cos_t - k2 * sin_t, k1 * sin_t + k2 * cos_t,
         jnp.zeros((LANES - QK_ROPE, tm), F32)], axis=0)
    krot = krot_t.T.astype(BF16)

    kn = jnp.dot(ckvn, wk_ref[...], preferred_element_type=F32)
    vt = lax.dot_general(wvt_ref[...], ckvn, _NT, preferred_element_type=F32)
    for h in range(N_HEADS):
        k_ref[h, :, 0:QK_NOPE] = kn[:, h * QK_NOPE:(h + 1) * QK_NOPE].astype(BF16)
        k_ref[h, :, QK_NOPE:QK_DIM] = krot[:, :QK_ROPE]
        vt_ref[h, 0:V_HEAD, :] = vt[h * V_HEAD:(h + 1) * V_HEAD].astype(BF16)
        vt_ref[h, V_HEAD:V_ROWS, :] = jnp.ones((V_ROWS - V_HEAD, tm), BF16)


def _qkv(c_lat, pos_row, inv_freq, gq, gkv, wqt, wk, wvt, tm):
    s = c_lat.shape[0]
    scale = math.log2(math.e) / math.sqrt(QK_NOPE + QK_ROPE)
    assert Q_LORA == KV_LORA and (Q_LORA + KV_LORA) % LANES == 0
    return pl.pallas_call(
        functools.partial(_qkv_kernel, scale=scale),
        out_shape=(jax.ShapeDtypeStruct((N_HEADS, QK_DIM, s), BF16),
                   jax.ShapeDtypeStruct((N_HEADS, s, QK_DIM), BF16),
                   jax.ShapeDtypeStruct((N_HEADS, V_ROWS, s), BF16)),
        grid=(s // tm,),
        in_specs=[pl.BlockSpec((tm, Q_LORA), lambda i: (i, 0)),
                  pl.BlockSpec((tm, KV_LORA), lambda i: (i, 1)),
                  pl.BlockSpec((tm, LANES), lambda i: (i, (Q_LORA + KV_LORA) // LANES)),
                  pl.BlockSpec((1, tm), lambda i: (0, i)),
                  _resident((HALF_ROPE, 1), lambda i: (0, 0)),
                  _resident((1, Q_LORA), lambda i: (0, 0)),
                  _resident((1, KV_LORA), lambda i: (0, 0)),
                  _resident(wqt.shape, lambda i: (0, 0)),
                  _resident(wk.shape, lambda i: (0, 0)),
                  _resident(wvt.shape, lambda i: (0, 0))],
        out_specs=(pl.BlockSpec((N_HEADS, QK_DIM, tm), lambda i: (0, 0, i)),
                   pl.BlockSpec((N_HEADS, tm, QK_DIM), lambda i: (0, i, 0)),
                   pl.BlockSpec((N_HEADS, V_ROWS, tm), lambda i: (0, 0, i))),
        name="qkv", compiler_params=_cparams(("parallel",)),
    )(c_lat, c_lat, c_lat, pos_row, inv_freq, gq, gkv, wqt, wk, wvt)


_NEG = -1e30


ATTN_UNROLL = 8


def _attn_kernel(qt_ref, k_ref, vt_ref, o_ref, s_ref, mx_ref, acc_ref, m_ref, *, t):
    seq = k_ref.shape[1]
    tq = 2 * t
    krow = lax.broadcasted_iota(jnp.int32, (t, t), 0)
    qcol = lax.broadcasted_iota(jnp.int32, (t, t), 1)
    causal = krow <= qcol

    def pv(ks, p):
        return jnp.dot(vt_ref[0, :, pl.ds(ks, t)], p, preferred_element_type=F32)

    def put_scores(slot, ks, q_start):
        s = jnp.dot(k_ref[0, pl.ds(ks, t), :], qt_ref[0, :, pl.ds(q_start, tq)],
                    preferred_element_type=F32)
        s_ref[slot] = s
        mx_ref[slot] = jnp.max(s, axis=0, keepdims=True)

    def query_tile(i, carry):
        kd = pl.multiple_of(i * tq, tq)
        n_full = 2 * i

        def qk(ks, q0, nq):
            return jnp.dot(k_ref[0, pl.ds(ks, t), :], qt_ref[0, :, pl.ds(kd + q0, nq)],
                           preferred_element_type=F32)

        m_ref[...] = jnp.full_like(m_ref, _NEG)
        acc_ref[...] = jnp.zeros_like(acc_ref)

        def item(slot, j):
            put_scores(1 - slot, pl.multiple_of((j + 1) * t, t), kd)
            m_old = m_ref[...]
            m_new = jnp.maximum(m_old, mx_ref[slot])
            p = jnp.exp2(s_ref[slot] - m_new).astype(BF16)
            acc_ref[...] = jnp.exp2(m_old - m_new) * acc_ref[...] + pv(pl.multiple_of(j * t, t), p)
            m_ref[...] = m_new

        def run(width, j0):
            for u in range(width):
                item(u % 2, j0 + u)

        done = 0
        width = 2
        while width < ATTN_UNROLL:
            bit = lax.rem(n_full // width, 2)
            pl.when(bit == 1)(functools.partial(run, width, done))
            done = done + bit * width
            width *= 2

        def main(r, c):
            run(ATTN_UNROLL, done + ATTN_UNROLL * r)
            return c

        lax.fori_loop(0, n_full // ATTN_UNROLL, main, 0)

        s1 = jnp.where(causal, qk(kd + t, t, t), _NEG)
        s0 = s_ref[0]
        s0 = jnp.concatenate([jnp.where(causal, s0[:, :t], _NEG), s0[:, t:]], axis=1)
        put_scores(0, 0, pl.multiple_of(jnp.minimum(kd + tq, seq - tq), tq))
        m_old = m_ref[...]
        m0 = jnp.maximum(m_old, jnp.max(s0, axis=0, keepdims=True))
        acc0 = jnp.exp2(m_old - m0) * acc_ref[...] + pv(kd, jnp.exp2(s0 - m0).astype(BF16))
        m0r = m0[:, t:]
        m1 = jnp.maximum(m0r, jnp.max(s1, axis=0, keepdims=True))
        acc1 = jnp.exp2(m0r - m1) * acc0[:, t:] + pv(kd + t, jnp.exp2(s1 - m1).astype(BF16))
        acc = jnp.concatenate([acc0[:, :t], acc1], axis=1)
        o_ref[pl.ds(kd, tq), :] = (acc[:V_HEAD] * (1.0 / acc[V_HEAD:V_HEAD + 1])).T.astype(BF16)
        return carry

    put_scores(0, 0, 0)
    lax.fori_loop(0, seq // tq, query_tile, 0)


def _attention(qt, k, vt, t):
    s = k.shape[1]
    head = lambda h: (h, 0, 0)
    return pl.pallas_call(
        functools.partial(_attn_kernel, t=t),
        out_shape=jax.ShapeDtypeStruct((s, N_HEADS * V_HEAD), BF16),
        grid=(N_HEADS,),
        in_specs=[pl.BlockSpec((1, QK_DIM, s), head),
                  pl.BlockSpec((1, s, QK_DIM), head),
                  pl.BlockSpec((1, V_ROWS, s), head)],
        out_specs=pl.BlockSpec((s, V_HEAD), lambda h: (0, h), pipeline_mode=pl.Buffered(1)),
        scratch_shapes=[pltpu.VMEM((2, t, 2 * t), F32), pltpu.VMEM((2, 1, 2 * t), F32),
                        pltpu.VMEM((V_ROWS, 2 * t), F32), pltpu.VMEM((1, 2 * t), F32)],
        name="attn", compiler_params=_cparams(("parallel",)),
    )(qt, k, vt)


def _layer_norm(r, g, b):
    mu = jnp.mean(r, axis=-1, keepdims=True)
    c = r - mu
    var = jnp.mean(c * c, axis=-1, keepdims=True)
    return c * lax.rsqrt(var + LN_EPS) * g + b


def _outproj_kernel(sg_ref, gm_ref, mla_ref, x_ref, w_ref, g_ref, b_ref, h_ref, hb_ref):
    merged = sg_ref[...].astype(F32) + gm_ref[...].astype(F32) * mla_ref[...].astype(F32)
    mix = jnp.dot(merged.astype(BF16), w_ref[...], preferred_element_type=F32)
    h = _layer_norm(DEEPNORM_ALPHA * x_ref[...] + mix, g_ref[...], b_ref[...])
    h_ref[...] = h
    hb_ref[...] = h.astype(BF16)


def _outproj(ssm_gated, gates, mla, x2, w_out, ln_g, ln_b, tm):
    s = x2.shape[0]
    row = lambda i: (i, 0)
    return pl.pallas_call(
        _outproj_kernel,
        out_shape=(jax.ShapeDtypeStruct((s, D_MODEL), F32),
                   jax.ShapeDtypeStruct((s, D_MODEL), BF16)),
        grid=(s // tm,),
        in_specs=[pl.BlockSpec((tm, D_MODEL), row),
                  pl.BlockSpec((tm, D_MODEL), lambda i: (i, 1)),
                  pl.BlockSpec((tm, D_MODEL), row),
                  pl.BlockSpec((tm, D_MODEL), row),
                  _resident(w_out.shape, lambda i: (0, 0)),
                  _resident((1, D_MODEL), lambda i: (0, 0)),
                  _resident((1, D_MODEL), lambda i: (0, 0))],
        out_specs=(pl.BlockSpec((tm, D_MODEL), row), pl.BlockSpec((tm, D_MODEL), row)),
        name="outproj", compiler_params=_cparams(("parallel",)),
    )(ssm_gated, gates, mla, x2, w_out, ln_g, ln_b)


FFN_RESIDUAL_CHUNKS = 8


def _ffn_kernel(hb_ref, h_ref, wg_ref, wu_ref, wd_ref, g_ref, b_ref, o_ref):
    j = pl.program_id(1)
    chunk = h_ref.shape[0]

    @pl.when(j == 0)
    def _():
        o_ref[...] = jnp.zeros_like(o_ref)

    hb = hb_ref[...]
    gate = jnp.dot(hb, wg_ref[...], preferred_element_type=F32)
    up = jnp.dot(hb, wu_ref[...], preferred_element_type=F32)
    act = (gate * jax.nn.sigmoid(gate) * up).astype(BF16)
    o_ref[...] += jnp.dot(act, wd_ref[...], preferred_element_type=F32)

    @pl.when(j < FFN_RESIDUAL_CHUNKS)
    def _():
        rows = pl.ds(pl.multiple_of(j * chunk, chunk), chunk)
        o_ref[rows, :] += DEEPNORM_ALPHA * h_ref[...]

    @pl.when(j == pl.num_programs(1) - 1)
    def _():
        o_ref[...] = _layer_norm(o_ref[...], g_ref[...], b_ref[...])


def _ffn(hb, h1, wg, wu, wd, ln_g, ln_b, tm, tf):
    s = h1.shape[0]
    dff = wg.shape[1]
    steps = dff // tf
    assert steps >= FFN_RESIDUAL_CHUNKS and tm % FFN_RESIDUAL_CHUNKS == 0
    chunk = tm // FFN_RESIDUAL_CHUNKS
    return pl.pallas_call(
        _ffn_kernel,
        out_shape=jax.ShapeDtypeStruct((s, D_MODEL), F32),
        grid=(s // tm, steps),
        in_specs=[pl.BlockSpec((tm, D_MODEL), lambda i, j: (i, 0)),
                  pl.BlockSpec((chunk, D_MODEL),
                               lambda i, j: (i * FFN_RESIDUAL_CHUNKS
                                             + jnp.minimum(j, FFN_RESIDUAL_CHUNKS - 1), 0)),
                  pl.BlockSpec((D_MODEL, tf), lambda i, j: (0, j)),
                  pl.BlockSpec((D_MODEL, tf), lambda i, j: (0, j)),
                  pl.BlockSpec((tf, D_MODEL), lambda i, j: (j, 0)),
                  _resident((1, D_MODEL), lambda i, j: (0, 0)),
                  _resident((1, D_MODEL), lambda i, j: (0, 0))],
        out_specs=pl.BlockSpec((tm, D_MODEL), lambda i, j: (i, 0)),
        name="ffn", compiler_params=_cparams(("parallel", "arbitrary")),
    )(hb, h1, wg, wu, wd, ln_g, ln_b)


def _tile(s, want):
    t = min(s, want)
    assert s % t == 0, (s, t)
    return t


def kernel(x, positions, w_in, ssm_lambda_re, ssm_lambda_im, ssm_log_dt, ssm_b_re, ssm_b_im,
           ssm_c_re, ssm_c_im, ssm_d, w_glu, q_norm_g, w_uq, kv_norm_g, w_ukv, w_out,
           ln1_g, ln1_b, w_ffn_gate, w_ffn_up, w_ffn_down, ln2_g, ln2_b):
    bsz, seq, d_model = x.shape
    assert bsz == 1 and d_model == D_MODEL and w_in.shape[0] == DEPTH
    x2 = x.reshape(seq, D_MODEL)
    pos_row = positions.reshape(1, seq)
    inv_freq = (1.0 / (ROPE_THETA ** (jnp.arange(0, QK_ROPE, 2, dtype=F32) / QK_ROPE))
                ).reshape(HALF_ROPE, 1)
    h = x2
    for l in range(DEPTH):
        lat_hi = SSM_WIDTH + Q_LORA + KV_LORA + QK_ROPE
        w_a = jnp.pad(w_in[l][:, :lat_hi], ((0, 0), (0, LANES - QK_ROPE))).astype(BF16)
        w_g = w_in[l][:, lat_hi:].astype(BF16)
        wqt = w_uq[l].T.astype(BF16)
        wkv = w_ukv[l].reshape(KV_LORA, N_HEADS, QK_NOPE + V_HEAD)
        wk = wkv[:, :, :QK_NOPE].reshape(KV_LORA, N_HEADS * QK_NOPE).astype(BF16)
        wvt = wkv[:, :, QK_NOPE:].reshape(KV_LORA, N_HEADS * V_HEAD).T.astype(BF16)

        u_slabs, c_lat = _inproj_a(h, w_a, _tile(seq, ROWS_INPROJ))
        gates = _inproj_g(h, w_g, _tile(seq, ROWS_INPROJ))

        m_op, p_op, q_op, a_tab = _ssm_prep(ssm_lambda_re[l], ssm_lambda_im[l], ssm_log_dt[l],
                                            ssm_b_re[l], ssm_b_im[l], ssm_c_re[l], ssm_c_im[l])
        d_tab = jnp.tile(ssm_d[l].reshape(N_SLABS, 1, LANES), (1, 1, SSM_T))
        y_slabs = _ssm_scan(u_slabs, m_op, p_op, q_op, a_tab, d_tab, _tile(seq, ROWS_SSM))
        ssm_gated = _glu(y_slabs, w_glu[l].astype(BF16), gates, _tile(seq, ROWS_GLU))

        qt, k, vt = _qkv(c_lat, pos_row, inv_freq, q_norm_g[l].reshape(1, Q_LORA),
                         kv_norm_g[l].reshape(1, KV_LORA), wqt, wk, wvt, _tile(seq, ROWS_QKV))
        mla = _attention(qt, k, vt, _tile(seq // 2, ROWS_ATTN))

        h, hb = _outproj(ssm_gated, gates, mla, h, w_out[l].astype(BF16),
                         ln1_g[l].reshape(1, D_MODEL), ln1_b[l].reshape(1, D_MODEL),
                         _tile(seq, ROWS_OUTPROJ))

        h = _ffn(hb, h, w_ffn_gate[l].astype(BF16), w_ffn_up[l].astype(BF16),
                 w_ffn_down[l].astype(BF16), ln2_g[l].reshape(1, D_MODEL),
                 ln2_b[l].reshape(1, D_MODEL), _tile(seq, ROWS_FFN), COLS_FFN)
    return h.reshape(bsz, seq, D_MODEL)
```

```python
import functools
import math

import jax
import jax.numpy as jnp
from jax import lax
from jax.experimental import pallas as pl
from jax.experimental.pallas import tpu as pltpu

F32 = jnp.float32
BF16 = jnp.bfloat16

D_MODEL = 2048
SSM_GROUP = 16
SSM_WIDTH = D_MODEL // 2
SSM_GROUPS = SSM_WIDTH // SSM_GROUP
SSM_STATE = 64
N_HEADS = 16
QK_NOPE = 128
QK_ROPE = 64
V_HEAD = 128
Q_LORA = 512
KV_LORA = 512
ROPE_THETA = 10000.0
DEPTH = 1
DEEPNORM_ALPHA = (2.0 * DEPTH) ** 0.25
LN_EPS = 1e-5
RMS_EPS = 1e-6

LANES = 128
BF16_SUBLANES = 16
V7X_MXU_DIM = 256
V7X_VMEM_BYTES = 64 * 1024 * 1024
VMEM_LIMIT = V7X_VMEM_BYTES - 8 * 1024 * 1024

HALF_ROPE = QK_ROPE // 2
QK_DIM = QK_NOPE + QK_ROPE
assert QK_DIM <= V7X_MXU_DIM and QK_DIM % BF16_SUBLANES == 0
SSM_T = 8
SLAB_GROUPS = LANES // SSM_GROUP
N_SLABS = SSM_WIDTH // LANES
SLAB_K = SSM_T * LANES
SLAB_STATE = SLAB_GROUPS * 2 * SSM_STATE
HALF_STATE = SLAB_STATE // 2
V_ROWS = V_HEAD + BF16_SUBLANES

ROWS_INPROJ = 512
ROWS_SSM = 8192
ROWS_GLU = 512
ROWS_QKV = 512
ROWS_ATTN = 512
ROWS_OUTPROJ = 512
ROWS_FFN = 1024
COLS_FFN = 512


def _cparams(sem, vmem=VMEM_LIMIT):
    return pltpu.CompilerParams(dimension_semantics=sem, vmem_limit_bytes=vmem)


def _resident(shape, index_map):
    return pl.BlockSpec(shape, index_map, pipeline_mode=pl.Buffered(1))


def _ssm_prep_kernel(lre_ref, lim_ref, ldt_ref, btr_ref, bti_ref, cr_ref, ci_ref,
                     lre_flat_ref, lim_flat_ref, ldt_flat_ref, m_ref, p_ref, q_ref, a_ref):
    gl, n, pp, t_len = SLAB_GROUPS, SSM_STATE, SSM_GROUP, SSM_T
    lre = lre_ref[...]
    lim = lim_ref[...]
    dt = jnp.exp(ldt_ref[...])

    def apow(xr, xi, k):
        mag = jnp.exp(xr * float(k))
        ang = xi * float(k)
        return mag * jnp.cos(ang), mag * jnp.sin(ang)

    xr = lre * dt
    xi = lim * dt
    ar, ai = apow(xr, xi, 1)
    den = lre * lre + lim * lim
    nr = ar - 1.0
    coef_re = (nr * lre + ai * lim) / den
    coef_im = (ai * lre - nr * lim) / den
    btr = btr_ref[...]
    bti = bti_ref[...]
    bbr = coef_re * btr - coef_im * bti
    bbi = coef_re * bti + coef_im * btr
    cr = cr_ref[...]
    ci = ci_ref[...]

    def spread(width, period):
        r = lax.broadcasted_iota(jnp.int32, (period, width), 0)
        c = lax.broadcasted_iota(jnp.int32, (period, width), 1)
        return (c % period == r).astype(BF16)

    def same_group(rows, row_period, cols, col_period):
        r = lax.broadcasted_iota(jnp.int32, (rows, cols), 0)
        c = lax.broadcasted_iota(jnp.int32, (rows, cols), 1)
        return r // row_period == c // col_period

    rep_p = spread(LANES, pp)
    rep_n = spread(HALF_STATE, n)
    mask_pp = same_group(LANES, pp, LANES, pp)
    mask_pn = same_group(LANES, pp, HALF_STATE, n)

    def block_diag(x, rep, mask):
        x2 = x.reshape(gl * pp, x.shape[-1]).astype(BF16)
        return jnp.where(mask, jnp.dot(x2, rep, preferred_element_type=F32), 0.0)

    m_ref[...] = jnp.zeros_like(m_ref)
    dn = (((2,), (2,)), ((0,), (0,)))
    for k in range(t_len + 1):
        pr, pi = (jnp.ones_like(xr), jnp.zeros_like(xr)) if k == 0 else apow(xr, xi, k)
        car = cr * pr - ci * pi
        cai = cr * pi + ci * pr
        if k < t_len:
            resp = (lax.dot_general(bbr, car, dn, precision=lax.Precision.HIGHEST,
                                    preferred_element_type=F32)
                    - lax.dot_general(bbi, cai, dn, precision=lax.Precision.HIGHEST,
                                      preferred_element_type=F32))
            tile = block_diag(resp, rep_p, mask_pp).astype(BF16)
            for t0 in range(t_len - k):
                m_ref[0, t0 * LANES:(t0 + 1) * LANES, (t0 + k) * LANES:(t0 + k + 1) * LANES] = tile
            t0 = t_len - 1 - k
            for ri, inj in enumerate((pr * bbr - pi * bbi, pr * bbi + pi * bbr)):
                p_ref[0, t0 * LANES:(t0 + 1) * LANES, ri * HALF_STATE:(ri + 1) * HALF_STATE] = (
                    block_diag(inj, rep_n, mask_pn).astype(BF16))
        if k >= 1:
            for ri, ca in enumerate((car, -cai)):
                z = block_diag(ca, rep_n, mask_pn)
                q_ref[0, ri * HALF_STATE:(ri + 1) * HALF_STATE, (k - 1) * LANES:k * LANES] = (
                    z.T.astype(BF16))

    dtf = jnp.exp(ldt_flat_ref[0])
    xrf = lre_flat_ref[0] * dtf
    xif = lim_flat_ref[0] * dtf
    for j in range(1, t_len + 1):
        pr, pi = apow(xrf, xif, t_len * j)
        a_ref[0, j - 1:j, 0:HALF_STATE] = pr
        a_ref[0, j - 1:j, HALF_STATE:] = pi


def _ssm_prep(lam_re, lam_im, log_dt, b_re, b_im, c_re, c_im):
    g, n, p, s = SSM_GROUPS, SSM_STATE, SSM_GROUP, N_SLABS
    gl = SLAB_GROUPS
    grp = lambda shape: pl.BlockSpec((gl,) + shape, lambda i: (i, 0, 0))
    flat = pl.BlockSpec((1, 1, HALF_STATE), lambda i: (i, 0, 0))
    op = pl.BlockSpec((1, SLAB_K, SLAB_K), lambda i: (i, 0, 0))
    return pl.pallas_call(
        _ssm_prep_kernel,
        out_shape=(jax.ShapeDtypeStruct((s, SLAB_K, SLAB_K), BF16),
                   jax.ShapeDtypeStruct((s, SLAB_K, SLAB_STATE), BF16),
                   jax.ShapeDtypeStruct((s, SLAB_STATE, SLAB_K), BF16),
                   jax.ShapeDtypeStruct((s, SSM_T, SLAB_STATE), F32)),
        grid=(s,),
        in_specs=[grp((1, n)), grp((1, n)), grp((1, 1)), grp((p, n)), grp((p, n)),
                  grp((p, n)), grp((p, n)), flat, flat, flat],
        out_specs=(op, op, op, pl.BlockSpec((1, SSM_T, SLAB_STATE), lambda i: (i, 0, 0))),
        name="ssm_prep", compiler_params=_cparams(("parallel",)),
    )(lam_re.reshape(g, 1, n), lam_im.reshape(g, 1, n), log_dt.reshape(g, 1, 1),
      jnp.swapaxes(b_re, 1, 2), jnp.swapaxes(b_im, 1, 2), c_re, c_im,
      lam_re.reshape(s, 1, HALF_STATE), lam_im.reshape(s, 1, HALF_STATE),
      jnp.repeat(log_dt, n).reshape(s, 1, HALF_STATE))


def _inproj_a_kernel(x_ref, w_ref, u_ref, c_ref):
    z = jnp.dot(x_ref[...].astype(BF16), w_ref[...], preferred_element_type=F32)
    for k in range(N_SLABS):
        u_ref[k] = z[:, k * LANES:(k + 1) * LANES]
    c_ref[...] = z[:, SSM_WIDTH:]


def _inproj_a(x2, w_a, tm):
    s = x2.shape[0]
    nc = w_a.shape[1] - SSM_WIDTH
    return pl.pallas_call(
        _inproj_a_kernel,
        out_shape=(jax.ShapeDtypeStruct((N_SLABS, s, LANES), F32),
                   jax.ShapeDtypeStruct((s, nc), F32)),
        grid=(s // tm,),
        in_specs=[pl.BlockSpec((tm, D_MODEL), lambda i: (i, 0)),
                  _resident(w_a.shape, lambda i: (0, 0))],
        out_specs=(pl.BlockSpec((N_SLABS, tm, LANES), lambda i: (0, i, 0)),
                   pl.BlockSpec((tm, nc), lambda i: (i, 0))),
        name="inproj_a", compiler_params=_cparams(("parallel",)),
    )(x2, w_a)


def _inproj_g_kernel(x_ref, w_ref, g_ref):
    z = jnp.dot(x_ref[...].astype(BF16), w_ref[...], preferred_element_type=F32)
    g_ref[...] = jax.nn.sigmoid(z).astype(BF16)


def _inproj_g(x2, w_g, tm):
    s = x2.shape[0]
    ng = w_g.shape[1]
    return pl.pallas_call(
        _inproj_g_kernel,
        out_shape=jax.ShapeDtypeStruct((s, ng), BF16),
        grid=(s // tm,),
        in_specs=[pl.BlockSpec((tm, D_MODEL), lambda i: (i, 0)),
                  _resident(w_g.shape, lambda i: (0, 0))],
        out_specs=pl.BlockSpec((tm, ng), lambda i: (i, 0)),
        name="inproj_g", compiler_params=_cparams(("parallel",)),
    )(x2, w_g)


def _cmul(ar, ai, br, bi):
    return ar * br - ai * bi, ar * bi + ai * br


def _ssm_scan_kernel(u_ref, m_ref, p_ref, q_ref, a_ref, d_ref, y_ref, hs_ref, carry_ref, *, rows):
    tb = pl.program_id(1)

    @pl.when(tb == 0)
    def _():
        carry_ref[...] = jnp.zeros_like(carry_ref)

    v32 = jnp.concatenate(
        [u_ref[0, pl.ds(t, rows, stride=SSM_T), :] for t in range(SSM_T)], axis=-1)
    vb = v32.astype(BF16)
    x_inj = jnp.dot(vb, p_ref[0], preferred_element_type=F32)
    y_intra = jnp.dot(vb, m_ref[0], preferred_element_type=F32)

    a_tab = a_ref[0]
    row = lax.broadcasted_iota(jnp.int32, (rows, HALF_STATE), 0)
    xs = pltpu.roll(x_inj, 1, 0)
    first = row == 0
    tiles = rows // SSM_T
    re = jnp.where(first, carry_ref[:, :HALF_STATE][0:1], xs[:, :HALF_STATE]).reshape(
        tiles, SSM_T, HALF_STATE)
    im = jnp.where(first, carry_ref[:, HALF_STATE:][0:1], xs[:, HALF_STATE:]).reshape(
        tiles, SSM_T, HALF_STATE)
    sub = lax.broadcasted_iota(jnp.int32, (SSM_T, HALF_STATE), 0)
    for d in (1, 2, 4):
        ar = jnp.where(sub >= d, a_tab[d - 1:d, :HALF_STATE], 0.0)
        ai = jnp.where(sub >= d, a_tab[d - 1:d, HALF_STATE:], 0.0)
        pr, pi = _cmul(ar, ai, pltpu.roll(re, d, 1), pltpu.roll(im, d, 1))
        re = re + pr
        im = im + pi
    hs_ref[:, :HALF_STATE] = re.reshape(rows, HALF_STATE)
    hs_ref[:, HALF_STATE:] = im.reshape(rows, HALF_STATE)

    tab_r = a_tab[:, :HALF_STATE]
    tab_i = a_tab[:, HALF_STATE:]

    def tile_body(k, last):
        lr, li = last
        r0 = pl.multiple_of(k * SSM_T, SSM_T)
        cr, ci = _cmul(tab_r, tab_i, lr, li)
        hr = hs_ref[pl.ds(r0, SSM_T), :HALF_STATE] + cr
        hi = hs_ref[pl.ds(r0, SSM_T), HALF_STATE:] + ci
        hs_ref[pl.ds(r0, SSM_T), :HALF_STATE] = hr
        hs_ref[pl.ds(r0, SSM_T), HALF_STATE:] = hi
        return (jnp.broadcast_to(hr[SSM_T - 1:SSM_T], (SSM_T, HALF_STATE)),
                jnp.broadcast_to(hi[SSM_T - 1:SSM_T], (SSM_T, HALF_STATE)))

    zero = jnp.zeros((SSM_T, HALF_STATE), F32)
    lr, li = lax.fori_loop(0, rows // SSM_T, tile_body, (zero, zero))

    nr, ni = _cmul(a_tab[0:1, :HALF_STATE], a_tab[0:1, HALF_STATE:], lr, li)
    x_last = x_inj[rows - 1:rows, :]
    carry_ref[:, :HALF_STATE] = nr + x_last[:, :HALF_STATE]
    carry_ref[:, HALF_STATE:] = ni + x_last[:, HALF_STATE:]

    y = (y_intra + jnp.dot(hs_ref[...].astype(BF16), q_ref[0], preferred_element_type=F32)
         + d_ref[0] * v32)
    for t in range(SSM_T):
        y_ref[0, pl.ds(t, rows, stride=SSM_T), :] = y[:, t * LANES:(t + 1) * LANES]


def _ssm_scan(u_slabs, m_op, p_op, q_op, a_tab, d_tab, tb_rows):
    s = u_slabs.shape[1]
    rows = tb_rows // SSM_T
    op_spec = pl.BlockSpec((1, SLAB_K, SLAB_K), lambda i, j: (i, 0, 0))
    return pl.pallas_call(
        functools.partial(_ssm_scan_kernel, rows=rows),
        out_shape=jax.ShapeDtypeStruct((N_SLABS, s, LANES), F32),
        grid=(N_SLABS, s // tb_rows),
        in_specs=[pl.BlockSpec((1, tb_rows, LANES), lambda i, j: (i, j, 0)),
                  op_spec, op_spec, op_spec,
                  pl.BlockSpec((1, SSM_T, SLAB_STATE), lambda i, j: (i, 0, 0)),
                  pl.BlockSpec((1, 1, SLAB_K), lambda i, j: (i, 0, 0))],
        out_specs=pl.BlockSpec((1, tb_rows, LANES), lambda i, j: (i, j, 0)),
        scratch_shapes=[pltpu.VMEM((rows, SLAB_STATE), F32),
                        pltpu.VMEM((SSM_T, SLAB_STATE), F32)],
        name="ssm_scan", compiler_params=_cparams(("parallel", "arbitrary")),
    )(u_slabs, m_op, p_op, q_op, a_tab, d_tab)


def _gelu_tanh(x):
    c = math.sqrt(2.0 / math.pi)
    return 0.5 * x * (1.0 + jnp.tanh(c * (x + 0.044715 * (x * x * x))))


def _glu_kernel(y_ref, w_ref, g_ref, o_ref):
    y = jnp.concatenate([y_ref[k] for k in range(N_SLABS)], axis=-1)
    z = jnp.dot(_gelu_tanh(y).astype(BF16), w_ref[...], preferred_element_type=F32)
    out = z[:, :D_MODEL] * jax.nn.sigmoid(z[:, D_MODEL:]) * g_ref[...].astype(F32)
    o_ref[...] = out.astype(BF16)


def _glu(y_slabs, w_glu, gates, tm):
    s = y_slabs.shape[1]
    return pl.pallas_call(
        _glu_kernel,
        out_shape=jax.ShapeDtypeStruct((s, D_MODEL), BF16),
        grid=(s // tm,),
        in_specs=[pl.BlockSpec((N_SLABS, tm, LANES), lambda i: (0, i, 0)),
                  _resident(w_glu.shape, lambda i: (0, 0)),
                  pl.BlockSpec((tm, D_MODEL), lambda i: (i, 0))],
        out_specs=pl.BlockSpec((tm, D_MODEL), lambda i: (i, 0)),
        name="glu", compiler_params=_cparams(("parallel",)),
    )(y_slabs, w_glu, gates)


def _rms(x, g):
    return x * lax.rsqrt(jnp.mean(x * x, axis=-1, keepdims=True) + RMS_EPS) * g


_NT = (((1,), (1,)), ((), ()))


def _qkv_kernel(cq_ref, ckv_ref, kr_ref, pos_ref, invf_ref, gq_ref, gkv_ref,
                wqt_ref, wk_ref, wvt_ref, qt_ref, k_ref, vt_ref, *, scale):
    tm = cq_ref.shape[0]
    cqn = _rms(cq_ref[...], gq_ref[...]).astype(BF16)
    ckvn = _rms(ckv_ref[...], gkv_ref[...]).astype(BF16)

    ang = invf_ref[...] * pos_ref[...].astype(F32)
    cos_t = jnp.cos(ang)
    sin_t = jnp.sin(ang)

    qt = lax.dot_general(wqt_ref[...], cqn, _NT, preferred_element_type=F32) * scale
    for h in range(N_HEADS):
        b = h * QK_DIM
        t1 = qt[b + QK_NOPE:b + QK_NOPE + HALF_ROPE]
        t2 = qt[b + QK_NOPE + HALF_ROPE:b + QK_DIM]
        qt_ref[h, 0:QK_NOPE, :] = qt[b:b + QK_NOPE].astype(BF16)
        qt_ref[h, QK_NOPE:QK_NOPE + HALF_ROPE, :] = (t1 * cos_t - t2 * sin_t).astype(BF16)
        qt_ref[h, QK_NOPE + HALF_ROPE:QK_DIM, :] = (t1 * sin_t + t2 * cos_t).astype(BF16)

    kr_t = kr_ref[...].T
    k1 = kr_t[0:HALF_ROPE]
    k2 = kr_t[HALF_ROPE:QK_ROPE]
    krot_t = jnp.concatenate(
        [k1 * cos_t - k2 * sin_t, k1 * sin_t + k2 * cos_t,
         jnp.zeros((LANES - QK_ROPE, tm), F32)], axis=0)
    krot = krot_t.T.astype(BF16)

    kn = jnp.dot(ckvn, wk_ref[...], preferred_element_type=F32)
    vt = lax.dot_general(wvt_ref[...], ckvn, _NT, preferred_element_type=F32)
    for h in range(N_HEADS):
        k_ref[h, :, 0:QK_NOPE] = kn[:, h * QK_NOPE:(h + 1) * QK_NOPE].astype(BF16)
        k_ref[h, :, QK_NOPE:QK_DIM] = krot[:, :QK_ROPE]
        vt_ref[h, 0:V_HEAD, :] = vt[h * V_HEAD:(h + 1) * V_HEAD].astype(BF16)
        vt_ref[h, V_HEAD:V_ROWS, :] = jnp.ones((V_ROWS - V_HEAD, tm), BF16)


def _qkv(c_lat, pos_row, inv_freq, gq, gkv, wqt, wk, wvt, tm):
    s = c_lat.shape[0]
    scale = math.log2(math.e) / math.sqrt(QK_NOPE + QK_ROPE)
    assert Q_LORA == KV_LORA and (Q_LORA + KV_LORA) % LANES == 0
    return pl.pallas_call(
        functools.partial(_qkv_kernel, scale=scale),
        out_shape=(jax.ShapeDtypeStruct((N_HEADS, QK_DIM, s), BF16),
                   jax.ShapeDtypeStruct((N_HEADS, s, QK_DIM), BF16),
                   jax.ShapeDtypeStruct((N_HEADS, V_ROWS, s), BF16)),
        grid=(s // tm,),
        in_specs=[pl.BlockSpec((tm, Q_LORA), lambda i: (i, 0)),
                  pl.BlockSpec((tm, KV_LORA), lambda i: (i, 1)),
                  pl.BlockSpec((tm, LANES), lambda i: (i, (Q_LORA + KV_LORA) // LANES)),
                  pl.BlockSpec((1, tm), lambda i: (0, i)),
                  _resident((HALF_ROPE, 1), lambda i: (0, 0)),
                  _resident((1, Q_LORA), lambda i: (0, 0)),
                  _resident((1, KV_LORA), lambda i: (0, 0)),
                  _resident(wqt.shape, lambda i: (0, 0)),
                  _resident(wk.shape, lambda i: (0, 0)),
                  _resident(wvt.shape, lambda i: (0, 0))],
        out_specs=(pl.BlockSpec((N_HEADS, QK_DIM, tm), lambda i: (0, 0, i)),
                   pl.BlockSpec((N_HEADS, tm, QK_DIM), lambda i: (0, i, 0)),
                   pl.BlockSpec((N_HEADS, V_ROWS, tm), lambda i: (0, 0, i))),
        name="qkv", compiler_params=_cparams(("parallel",)),
    )(c_lat, c_lat, c_lat, pos_row, inv_freq, gq, gkv, wqt, wk, wvt)


_NEG = -1e30


ATTN_UNROLL = 16


def _attn_kernel(qt_ref, k_ref, vt_ref, o_ref, s_ref, mx_ref, acc_ref, m_ref, *, t):
    seq = k_ref.shape[1]
    tq = 2 * t
    krow = lax.broadcasted_iota(jnp.int32, (t, t), 0)
    qcol = lax.broadcasted_iota(jnp.int32, (t, t), 1)
    causal = krow <= qcol

    def pv(ks, p):
        return jnp.dot(vt_ref[0, :, pl.ds(ks, t)], p, preferred_element_type=F32)

    def put_scores(slot, ks, q_start):
        s = jnp.dot(k_ref[0, pl.ds(ks, t), :], qt_ref[0, :, pl.ds(q_start, tq)],
                    preferred_element_type=F32)
        s_ref[slot] = s
        mx_ref[slot] = jnp.max(s, axis=0, keepdims=True)

    def query_tile(i, carry):
        kd = pl.multiple_of(i * tq, tq)
        n_full = 2 * i

        def qk(ks, q0, nq):
            return jnp.dot(k_ref[0, pl.ds(ks, t), :], qt_ref[0, :, pl.ds(kd + q0, nq)],
                           preferred_element_type=F32)

        m_ref[...] = jnp.full_like(m_ref, _NEG)
        acc_ref[...] = jnp.zeros_like(acc_ref)

        def item(slot, j):
            put_scores(1 - slot, pl.multiple_of((j + 1) * t, t), kd)
            m_old = m_ref[...]
            m_new = jnp.maximum(m_old, mx_ref[slot])
            p = jnp.exp2(s_ref[slot] - m_new).astype(BF16)
            acc_ref[...] = jnp.exp2(m_old - m_new) * acc_ref[...] + pv(pl.multiple_of(j * t, t), p)
            m_ref[...] = m_new

        def run(width, j0):
            for u in range(width):
                item(u % 2, j0 + u)

        done = 0
        width = 2
        while width < ATTN_UNROLL:
            bit = lax.rem(n_full // width, 2)
            pl.when(bit == 1)(functools.partial(run, width, done))
            done = done + bit * width
            width *= 2

        def main(r, c):
            run(ATTN_UNROLL, done + ATTN_UNROLL * r)
            return c

        lax.fori_loop(0, n_full // ATTN_UNROLL, main, 0)

        s1 = jnp.where(causal, qk(kd + t, t, t), _NEG)
        s0 = s_ref[0]
        s0 = jnp.concatenate([jnp.where(causal, s0[:, :t], _NEG), s0[:, t:]], axis=1)
        put_scores(0, 0, pl.multiple_of(jnp.minimum(kd + tq, seq - tq), tq))
        m_old = m_ref[...]
        m0 = jnp.maximum(m_old, jnp.max(s0, axis=0, keepdims=True))
        acc0 = jnp.exp2(m_old - m0) * acc_ref[...] + pv(kd, jnp.exp2(s0 - m0).astype(BF16))
        m0r = m0[:, t:]
        m1 = jnp.maximum(m0r, jnp.max(s1, axis=0, keepdims=True))
        acc1 = jnp.exp2(m0r - m1) * acc0[:, t:] + pv(kd + t, jnp.exp2(s1 - m1).astype(BF16))
        acc = jnp.concatenate([acc0[:, :t], acc1], axis=1)
        o_ref[pl.ds(kd, tq), :] = (acc[:V_HEAD] * (1.0 / acc[V_HEAD:V_HEAD + 1])).T.astype(BF16)
        return carry

    put_scores(0, 0, 0)
    lax.fori_loop(0, seq // tq, query_tile, 0)


def _attention(qt, k, vt, t):
    s = k.shape[1]
    head = lambda h: (h, 0, 0)
    return pl.pallas_call(
        functools.partial(_attn_kernel, t=t),
        out_shape=jax.ShapeDtypeStruct((s, N_HEADS * V_HEAD), BF16),
        grid=(N_HEADS,),
        in_specs=[pl.BlockSpec((1, QK_DIM, s), head),
                  pl.BlockSpec((1, s, QK_DIM), head),
                  pl.BlockSpec((1, V_ROWS, s), head)],
        out_specs=pl.BlockSpec((s, V_HEAD), lambda h: (0, h), pipeline_mode=pl.Buffered(1)),
        scratch_shapes=[pltpu.VMEM((2, t, 2 * t), F32), pltpu.VMEM((2, 1, 2 * t), F32),
                        pltpu.VMEM((V_ROWS, 2 * t), F32), pltpu.VMEM((1, 2 * t), F32)],
        name="attn", compiler_params=_cparams(("parallel",)),
    )(qt, k, vt)


def _layer_norm(r, g, b):
    mu = jnp.mean(r, axis=-1, keepdims=True)
    c = r - mu
    var = jnp.mean(c * c, axis=-1, keepdims=True)
    return c * lax.rsqrt(var + LN_EPS) * g + b


def _outproj_kernel(sg_ref, gm_ref, mla_ref, x_ref, w_ref, g_ref, b_ref, h_ref, hb_ref):
    merged = sg_ref[...].astype(F32) + gm_ref[...].astype(F32) * mla_ref[...].astype(F32)
    mix = jnp.dot(merged.astype(BF16), w_ref[...], preferred_element_type=F32)
    h = _layer_norm(DEEPNORM_ALPHA * x_ref[...] + mix, g_ref[...], b_ref[...])
    h_ref[...] = h
    hb_ref[...] = h.astype(BF16)


def _outproj(ssm_gated, gates, mla, x2, w_out, ln_g, ln_b, tm):
    s = x2.shape[0]
    row = lambda i: (i, 0)
    return pl.pallas_call(
        _outproj_kernel,
        out_shape=(jax.ShapeDtypeStruct((s, D_MODEL), F32),
                   jax.ShapeDtypeStruct((s, D_MODEL), BF16)),
        grid=(s // tm,),
        in_specs=[pl.BlockSpec((tm, D_MODEL), row),
                  pl.BlockSpec((tm, D_MODEL), lambda i: (i, 1)),
                  pl.BlockSpec((tm, D_MODEL), row),
                  pl.BlockSpec((tm, D_MODEL), row),
                  _resident(w_out.shape, lambda i: (0, 0)),
                  _resident((1, D_MODEL), lambda i: (0, 0)),
                  _resident((1, D_MODEL), lambda i: (0, 0))],
        out_specs=(pl.BlockSpec((tm, D_MODEL), row), pl.BlockSpec((tm, D_MODEL), row)),
        name="outproj", compiler_params=_cparams(("parallel",)),
    )(ssm_gated, gates, mla, x2, w_out, ln_g, ln_b)


FFN_RESIDUAL_CHUNKS = 8


def _ffn_kernel(hb_ref, h_ref, wg_ref, wu_ref, wd_ref, g_ref, b_ref, o_ref):
    j = pl.program_id(1)
    chunk = h_ref.shape[0]

    @pl.when(j == 0)
    def _():
        o_ref[...] = jnp.zeros_like(o_ref)

    hb = hb_ref[...]
    gate = jnp.dot(hb, wg_ref[...], preferred_element_type=F32)
    up = jnp.dot(hb, wu_ref[...], preferred_element_type=F32)
    act = (gate * jax.nn.sigmoid(gate) * up).astype(BF16)
    o_ref[...] += jnp.dot(act, wd_ref[...], preferred_element_type=F32)

    @pl.when(j < FFN_RESIDUAL_CHUNKS)
    def _():
        rows = pl.ds(pl.multiple_of(j * chunk, chunk), chunk)
        o_ref[rows, :] += DEEPNORM_ALPHA * h_ref[...]

    @pl.when(j == pl.num_programs(1) - 1)
    def _():
        o_ref[...] = _layer_norm(o_ref[...], g_ref[...], b_ref[...])


def _ffn(hb, h1, wg, wu, wd, ln_g, ln_b, tm, tf):
    s = h1.shape[0]
    dff = wg.shape[1]
    steps = dff // tf
    assert steps >= FFN_RESIDUAL_CHUNKS and tm % FFN_RESIDUAL_CHUNKS == 0
    chunk = tm // FFN_RESIDUAL_CHUNKS
    return pl.pallas_call(
        _ffn_kernel,
        out_shape=jax.ShapeDtypeStruct((s, D_MODEL), F32),
        grid=(s // tm, steps),
        in_specs=[pl.BlockSpec((tm, D_MODEL), lambda i, j: (i, 0)),
                  pl.BlockSpec((chunk, D_MODEL),
                               lambda i, j: (i * FFN_RESIDUAL_CHUNKS
                                             + jnp.minimum(j, FFN_RESIDUAL_CHUNKS - 1), 0)),
                  pl.BlockSpec((D_MODEL, tf), lambda i, j: (0, j)),
                  pl.BlockSpec((D_MODEL, tf), lambda i, j: (0, j)),
                  pl.BlockSpec((tf, D_MODEL), lambda i, j: (j, 0)),
                  _resident((1, D_MODEL), lambda i, j: (0, 0)),
                  _resident((1, D_MODEL), lambda i, j: (0, 0))],
        out_specs=pl.BlockSpec((tm, D_MODEL), lambda i, j: (i, 0)),
        name="ffn", compiler_params=_cparams(("parallel", "arbitrary")),
    )(hb, h1, wg, wu, wd, ln_g, ln_b)


def _tile(s, want):
    t = min(s, want)
    assert s % t == 0, (s, t)
    return t


def kernel(x, positions, w_in, ssm_lambda_re, ssm_lambda_im, ssm_log_dt, ssm_b_re, ssm_b_im,
           ssm_c_re, ssm_c_im, ssm_d, w_glu, q_norm_g, w_uq, kv_norm_g, w_ukv, w_out,
           ln1_g, ln1_b, w_ffn_gate, w_ffn_up, w_ffn_down, ln2_g, ln2_b):
    bsz, seq, d_model = x.shape
    assert bsz == 1 and d_model == D_MODEL and w_in.shape[0] == DEPTH
    x2 = x.reshape(seq, D_MODEL)
    pos_row = positions.reshape(1, seq)
    inv_freq = (1.0 / (ROPE_THETA ** (jnp.arange(0, QK_ROPE, 2, dtype=F32) / QK_ROPE))
                ).reshape(HALF_ROPE, 1)
    h = x2
    for l in range(DEPTH):
        lat_hi = SSM_WIDTH + Q_LORA + KV_LORA + QK_ROPE
        w_a = jnp.pad(w_in[l][:, :lat_hi], ((0, 0), (0, LANES - QK_ROPE))).astype(BF16)
        w_g = w_in[l][:, lat_hi:].astype(BF16)
        wqt = w_uq[l].T.astype(BF16)
        wkv = w_ukv[l].reshape(KV_LORA, N_HEADS, QK_NOPE + V_HEAD)
        wk = wkv[:, :, :QK_NOPE].reshape(KV_LORA, N_HEADS * QK_NOPE).astype(BF16)
        wvt = wkv[:, :, QK_NOPE:].reshape(KV_LORA, N_HEADS * V_HEAD).T.astype(BF16)

        u_slabs, c_lat = _inproj_a(h, w_a, _tile(seq, ROWS_INPROJ))
        gates = _inproj_g(h, w_g, _tile(seq, ROWS_INPROJ))

        m_op, p_op, q_op, a_tab = _ssm_prep(ssm_lambda_re[l], ssm_lambda_im[l], ssm_log_dt[l],
                                            ssm_b_re[l], ssm_b_im[l], ssm_c_re[l], ssm_c_im[l])
        d_tab = jnp.tile(ssm_d[l].reshape(N_SLABS, 1, LANES), (1, 1, SSM_T))
        y_slabs = _ssm_scan(u_slabs, m_op, p_op, q_op, a_tab, d_tab, _tile(seq, ROWS_SSM))
        ssm_gated = _glu(y_slabs, w_glu[l].astype(BF16), gates, _tile(seq, ROWS_GLU))

        qt, k, vt = _qkv(c_lat, pos_row, inv_freq, q_norm_g[l].reshape(1, Q_LORA),
                         kv_norm_g[l].reshape(1, KV_LORA), wqt, wk, wvt, _tile(seq, ROWS_QKV))
        mla = _attention(qt, k, vt, _tile(seq // 2, ROWS_ATTN))

        h, hb = _outproj(ssm_gated, gates, mla, h, w_out[l].astype(BF16),
                         ln1_g[l].reshape(1, D_MODEL), ln1_b[l].reshape(1, D_MODEL),
                         _tile(seq, ROWS_OUTPROJ))

        h = _ffn(hb, h, w_ffn_gate[l].astype(BF16), w_ffn_up[l].astype(BF16),
                 w_ffn_down[l].astype(BF16), ln2_g[l].reshape(1, D_MODEL),
                 ln2_b[l].reshape(1, D_MODEL), _tile(seq, ROWS_FFN), COLS_FFN)
    return h.reshape(bsz, seq, D_MODEL)
```

```python
import functools
import math

import jax
import jax.numpy as jnp
from jax import lax
from jax.experimental import pallas as pl
from jax.experimental.pallas import tpu as pltpu

F32 = jnp.float32
BF16 = jnp.bfloat16

D_MODEL = 2048
SSM_GROUP = 16
SSM_WIDTH = D_MODEL // 2
SSM_GROUPS = SSM_WIDTH // SSM_GROUP
SSM_STATE = 64
N_HEADS = 16
QK_NOPE = 128
QK_ROPE = 64
V_HEAD = 128
Q_LORA = 512
KV_LORA = 512
ROPE_THETA = 10000.0
DEPTH = 1
DEEPNORM_ALPHA = (2.0 * DEPTH) ** 0.25
LN_EPS = 1e-5
RMS_EPS = 1e-6

LANES = 128
BF16_SUBLANES = 16
V7X_MXU_DIM = 256
V7X_VMEM_BYTES = 64 * 1024 * 1024
VMEM_LIMIT = V7X_VMEM_BYTES - 8 * 1024 * 1024

HALF_ROPE = QK_ROPE // 2
QK_DIM = QK_NOPE + QK_ROPE
assert QK_DIM <= V7X_MXU_DIM and QK_DIM % BF16_SUBLANES == 0
SSM_T = 8
SLAB_GROUPS = LANES // SSM_GROUP
N_SLABS = SSM_WIDTH // LANES
SLAB_K = SSM_T * LANES
SLAB_STATE = SLAB_GROUPS * 2 * SSM_STATE
HALF_STATE = SLAB_STATE // 2
V_ROWS = V_HEAD + BF16_SUBLANES

ROWS_INPROJ = 512
ROWS_SSM = 8192
ROWS_GLU = 512
ROWS_QKV = 512
ROWS_ATTN = 512
ROWS_OUTPROJ = 512
ROWS_FFN = 1024
COLS_FFN = 512


def _cparams(sem, vmem=VMEM_LIMIT):
    return pltpu.CompilerParams(dimension_semantics=sem, vmem_limit_bytes=vmem)


def _resident(shape, index_map):
    return pl.BlockSpec(shape, index_map, pipeline_mode=pl.Buffered(1))


def _ssm_prep_kernel(lre_ref, lim_ref, ldt_ref, btr_ref, bti_ref, cr_ref, ci_ref,
                     lre_flat_ref, lim_flat_ref, ldt_flat_ref, m_ref, p_ref, q_ref, a_ref):
    gl, n, pp, t_len = SLAB_GROUPS, SSM_STATE, SSM_GROUP, SSM_T
    lre = lre_ref[...]
    lim = lim_ref[...]
    dt = jnp.exp(ldt_ref[...])

    def apow(xr, xi, k):
        mag = jnp.exp(xr * float(k))
        ang = xi * float(k)
        return mag * jnp.cos(ang), mag * jnp.sin(ang)

    xr = lre * dt
    xi = lim * dt
    ar, ai = apow(xr, xi, 1)
    den = lre * lre + lim * lim
    nr = ar - 1.0
    coef_re = (nr * lre + ai * lim) / den
    coef_im = (ai * lre - nr * lim) / den
    btr = btr_ref[...]
    bti = bti_ref[...]
    bbr = coef_re * btr - coef_im * bti
    bbi = coef_re * bti + coef_im * btr
    cr = cr_ref[...]
    ci = ci_ref[...]

    def spread(width, period):
        r = lax.broadcasted_iota(jnp.int32, (period, width), 0)
        c = lax.broadcasted_iota(jnp.int32, (period, width), 1)
        return (c % period == r).astype(BF16)

    def same_group(rows, row_period, cols, col_period):
        r = lax.broadcasted_iota(jnp.int32, (rows, cols), 0)
        c = lax.broadcasted_iota(jnp.int32, (rows, cols), 1)
        return r // row_period == c // col_period

    rep_p = spread(LANES, pp)
    rep_n = spread(HALF_STATE, n)
    mask_pp = same_group(LANES, pp, LANES, pp)
    mask_pn = same_group(LANES, pp, HALF_STATE, n)

    def block_diag(x, rep, mask):
        x2 = x.reshape(gl * pp, x.shape[-1]).astype(BF16)
        return jnp.where(mask, jnp.dot(x2, rep, preferred_element_type=F32), 0.0)

    m_ref[...] = jnp.zeros_like(m_ref)
    dn = (((2,), (2,)), ((0,), (0,)))
    for k in range(t_len + 1):
        pr, pi = (jnp.ones_like(xr), jnp.zeros_like(xr)) if k == 0 else apow(xr, xi, k)
        car = cr * pr - ci * pi
        cai = cr * pi + ci * pr
        if k < t_len:
            resp = (lax.dot_general(bbr, car, dn, precision=lax.Precision.HIGHEST,
                                    preferred_element_type=F32)
                    - lax.dot_general(bbi, cai, dn, precision=lax.Precision.HIGHEST,
                                      preferred_element_type=F32))
            tile = block_diag(resp, rep_p, mask_pp).astype(BF16)
            for t0 in range(t_len - k):
                m_ref[0, t0 * LANES:(t0 + 1) * LANES, (t0 + k) * LANES:(t0 + k + 1) * LANES] = tile
            t0 = t_len - 1 - k
            for ri, inj in enumerate((pr * bbr - pi * bbi, pr * bbi + pi * bbr)):
                p_ref[0, t0 * LANES:(t0 + 1) * LANES, ri * HALF_STATE:(ri + 1) * HALF_STATE] = (
                    block_diag(inj, rep_n, mask_pn).astype(BF16))
        if k >= 1:
            for ri, ca in enumerate((car, -cai)):
                z = block_diag(ca, rep_n, mask_pn)
                q_ref[0, ri * HALF_STATE:(ri + 1) * HALF_STATE, (k - 1) * LANES:k * LANES] = (
                    z.T.astype(BF16))

    dtf = jnp.exp(ldt_flat_ref[0])
    xrf = lre_flat_ref[0] * dtf
    xif = lim_flat_ref[0] * dtf
    for j in range(1, t_len + 1):
        pr, pi = apow(xrf, xif, t_len * j)
        a_ref[0, j - 1:j, 0:HALF_STATE] = pr
        a_ref[0, j - 1:j, HALF_STATE:] = pi


def _ssm_prep(lam_re, lam_im, log_dt, b_re, b_im, c_re, c_im):
    g, n, p, s = SSM_GROUPS, SSM_STATE, SSM_GROUP, N_SLABS
    gl = SLAB_GROUPS
    grp = lambda shape: pl.BlockSpec((gl,) + shape, lambda i: (i, 0, 0))
    flat = pl.BlockSpec((1, 1, HALF_STATE), lambda i: (i, 0, 0))
    op = pl.BlockSpec((1, SLAB_K, SLAB_K), lambda i: (i, 0, 0))
    return pl.pallas_call(
        _ssm_prep_kernel,
        out_shape=(jax.ShapeDtypeStruct((s, SLAB_K, SLAB_K), BF16),
                   jax.ShapeDtypeStruct((s, SLAB_K, SLAB_STATE), BF16),
                   jax.ShapeDtypeStruct((s, SLAB_STATE, SLAB_K), BF16),
                   jax.ShapeDtypeStruct((s, SSM_T, SLAB_STATE), F32)),
        grid=(s,),
        in_specs=[grp((1, n)), grp((1, n)), grp((1, 1)), grp((p, n)), grp((p, n)),
                  grp((p, n)), grp((p, n)), flat, flat, flat],
        out_specs=(op, op, op, pl.BlockSpec((1, SSM_T, SLAB_STATE), lambda i: (i, 0, 0))),
        name="ssm_prep", compiler_params=_cparams(("parallel",)),
    )(lam_re.reshape(g, 1, n), lam_im.reshape(g, 1, n), log_dt.reshape(g, 1, 1),
      jnp.swapaxes(b_re, 1, 2), jnp.swapaxes(b_im, 1, 2), c_re, c_im,
      lam_re.reshape(s, 1, HALF_STATE), lam_im.reshape(s, 1, HALF_STATE),
      jnp.repeat(log_dt, n).reshape(s, 1, HALF_STATE))


def _inproj_a_kernel(x_ref, w_ref, u_ref, c_ref):
    z = jnp.dot(x_ref[...].astype(BF16), w_ref[...], preferred_element_type=F32)
    for k in range(N_SLABS):
        u_ref[k] = z[:, k * LANES:(k + 1) * LANES]
    c_ref[...] = z[:, SSM_WIDTH:]


def _inproj_a(x2, w_a, tm):
    s = x2.shape[0]
    nc = w_a.shape[1] - SSM_WIDTH
    return pl.pallas_call(
        _inproj_a_kernel,
        out_shape=(jax.ShapeDtypeStruct((N_SLABS, s, LANES), F32),
                   jax.ShapeDtypeStruct((s, nc), F32)),
        grid=(s // tm,),
        in_specs=[pl.BlockSpec((tm, D_MODEL), lambda i: (i, 0)),
                  _resident(w_a.shape, lambda i: (0, 0))],
        out_specs=(pl.BlockSpec((N_SLABS, tm, LANES), lambda i: (0, i, 0)),
                   pl.BlockSpec((tm, nc), lambda i: (i, 0))),
        name="inproj_a", compiler_params=_cparams(("parallel",)),
    )(x2, w_a)


def _inproj_g_kernel(x_ref, w_ref, g_ref):
    z = jnp.dot(x_ref[...].astype(BF16), w_ref[...], preferred_element_type=F32)
    g_ref[...] = jax.nn.sigmoid(z).astype(BF16)


def _inproj_g(x2, w_g, tm):
    s = x2.shape[0]
    ng = w_g.shape[1]
    return pl.pallas_call(
        _inproj_g_kernel,
        out_shape=jax.ShapeDtypeStruct((s, ng), BF16),
        grid=(s // tm,),
        in_specs=[pl.BlockSpec((tm, D_MODEL), lambda i: (i, 0)),
                  _resident(w_g.shape, lambda i: (0, 0))],
        out_specs=pl.BlockSpec((tm, ng), lambda i: (i, 0)),
        name="inproj_g", compiler_params=_cparams(("parallel",)),
    )(x2, w_g)


def _cmul(ar, ai, br, bi):
    return ar * br - ai * bi, ar * bi + ai * br


def _ssm_scan_kernel(u_ref, m_ref, p_ref, q_ref, a_ref, d_ref, y_ref, hs_ref, carry_ref, *, rows):
    tb = pl.program_id(1)

    @pl.when(tb == 0)
    def _():
        carry_ref[...] = jnp.zeros_like(carry_ref)

    v32 = jnp.concatenate(
        [u_ref[0, pl.ds(t, rows, stride=SSM_T), :] for t in range(SSM_T)], axis=-1)
    vb = v32.astype(BF16)
    x_inj = jnp.dot(vb, p_ref[0], preferred_element_type=F32)
    blk = V7X_MXU_DIM
    y_intra = jnp.concatenate(
        [jnp.dot(vb[:, :(nb + 1) * blk], m_ref[0, :(nb + 1) * blk, nb * blk:(nb + 1) * blk],
                 preferred_element_type=F32) for nb in range(SLAB_K // blk)], axis=-1)

    a_tab = a_ref[0]
    row = lax.broadcasted_iota(jnp.int32, (rows, HALF_STATE), 0)
    xs = pltpu.roll(x_inj, 1, 0)
    first = row == 0
    tiles = rows // SSM_T
    re = jnp.where(first, carry_ref[:, :HALF_STATE][0:1], xs[:, :HALF_STATE]).reshape(
        tiles, SSM_T, HALF_STATE)
    im = jnp.where(first, carry_ref[:, HALF_STATE:][0:1], xs[:, HALF_STATE:]).reshape(
        tiles, SSM_T, HALF_STATE)
    sub = lax.broadcasted_iota(jnp.int32, (SSM_T, HALF_STATE), 0)
    for d in (1, 2, 4):
        ar = jnp.where(sub >= d, a_tab[d - 1:d, :HALF_STATE], 0.0)
        ai = jnp.where(sub >= d, a_tab[d - 1:d, HALF_STATE:], 0.0)
        pr, pi = _cmul(ar, ai, pltpu.roll(re, d, 1), pltpu.roll(im, d, 1))
        re = re + pr
        im = im + pi
    hs_ref[:, :HALF_STATE] = re.reshape(rows, HALF_STATE)
    hs_ref[:, HALF_STATE:] = im.reshape(rows, HALF_STATE)

    tab_r = a_tab[:, :HALF_STATE]
    tab_i = a_tab[:, HALF_STATE:]

    def tile_body(k, last):
        lr, li = last
        r0 = pl.multiple_of(k * SSM_T, SSM_T)
        cr, ci = _cmul(tab_r, tab_i, lr, li)
        hr = hs_ref[pl.ds(r0, SSM_T), :HALF_STATE] + cr
        hi = hs_ref[pl.ds(r0, SSM_T), HALF_STATE:] + ci
        hs_ref[pl.ds(r0, SSM_T), :HALF_STATE] = hr
        hs_ref[pl.ds(r0, SSM_T), HALF_STATE:] = hi
        return (jnp.broadcast_to(hr[SSM_T - 1:SSM_T], (SSM_T, HALF_STATE)),
                jnp.broadcast_to(hi[SSM_T - 1:SSM_T], (SSM_T, HALF_STATE)))

    zero = jnp.zeros((SSM_T, HALF_STATE), F32)
    lr, li = lax.fori_loop(0, rows // SSM_T, tile_body, (zero, zero))

    nr, ni = _cmul(a_tab[0:1, :HALF_STATE], a_tab[0:1, HALF_STATE:], lr, li)
    x_last = x_inj[rows - 1:rows, :]
    carry_ref[:, :HALF_STATE] = nr + x_last[:, :HALF_STATE]
    carry_ref[:, HALF_STATE:] = ni + x_last[:, HALF_STATE:]

    y = (y_intra + jnp.dot(hs_ref[...].astype(BF16), q_ref[0], preferred_element_type=F32)
         + d_ref[0] * v32)
    for t in range(SSM_T):
        y_ref[0, pl.ds(t, rows, stride=SSM_T), :] = y[:, t * LANES:(t + 1) * LANES]


def _ssm_scan(u_slabs, m_op, p_op, q_op, a_tab, d_tab, tb_rows):
    s = u_slabs.shape[1]
    rows = tb_rows // SSM_T
    op_spec = pl.BlockSpec((1, SLAB_K, SLAB_K), lambda i, j: (i, 0, 0))
    return pl.pallas_call(
        functools.partial(_ssm_scan_kernel, rows=rows),
        out_shape=jax.ShapeDtypeStruct((N_SLABS, s, LANES), F32),
        grid=(N_SLABS, s // tb_rows),
        in_specs=[pl.BlockSpec((1, tb_rows, LANES), lambda i, j: (i, j, 0)),
                  op_spec, op_spec, op_spec,
                  pl.BlockSpec((1, SSM_T, SLAB_STATE), lambda i, j: (i, 0, 0)),
                  pl.BlockSpec((1, 1, SLAB_K), lambda i, j: (i, 0, 0))],
        out_specs=pl.BlockSpec((1, tb_rows, LANES), lambda i, j: (i, j, 0)),
        scratch_shapes=[pltpu.VMEM((rows, SLAB_STATE), F32),
                        pltpu.VMEM((SSM_T, SLAB_STATE), F32)],
        name="ssm_scan", compiler_params=_cparams(("parallel", "arbitrary")),
    )(u_slabs, m_op, p_op, q_op, a_tab, d_tab)


def _gelu_tanh(x):
    c = math.sqrt(2.0 / math.pi)
    return 0.5 * x * (1.0 + jnp.tanh(c * (x + 0.044715 * (x * x * x))))


def _glu_kernel(y_ref, w_ref, g_ref, o_ref):
    y = jnp.concatenate([y_ref[k] for k in range(N_SLABS)], axis=-1)
    z = jnp.dot(_gelu_tanh(y).astype(BF16), w_ref[...], preferred_element_type=F32)
    out = z[:, :D_MODEL] * jax.nn.sigmoid(z[:, D_MODEL:]) * g_ref[...].astype(F32)
    o_ref[...] = out.astype(BF16)


def _glu(y_slabs, w_glu, gates, tm):
    s = y_slabs.shape[1]
    return pl.pallas_call(
        _glu_kernel,
        out_shape=jax.ShapeDtypeStruct((s, D_MODEL), BF16),
        grid=(s // tm,),
        in_specs=[pl.BlockSpec((N_SLABS, tm, LANES), lambda i: (0, i, 0)),
                  _resident(w_glu.shape, lambda i: (0, 0)),
                  pl.BlockSpec((tm, D_MODEL), lambda i: (i, 0))],
        out_specs=pl.BlockSpec((tm, D_MODEL), lambda i: (i, 0)),
        name="glu", compiler_params=_cparams(("parallel",)),
    )(y_slabs, w_glu, gates)


def _rms(x, g):
    return x * lax.rsqrt(jnp.mean(x * x, axis=-1, keepdims=True) + RMS_EPS) * g


_NT = (((1,), (1,)), ((), ()))


def _qkv_kernel(cq_ref, ckv_ref, kr_ref, pos_ref, invf_ref, gq_ref, gkv_ref,
                wqt_ref, wk_ref, wvt_ref, qt_ref, k_ref, vt_ref, *, scale):
    tm = cq_ref.shape[0]
    cqn = _rms(cq_ref[...], gq_ref[...]).astype(BF16)
    ckvn = _rms(ckv_ref[...], gkv_ref[...]).astype(BF16)

    ang = invf_ref[...] * pos_ref[...].astype(F32)
    cos_t = jnp.cos(ang)
    sin_t = jnp.sin(ang)

    qt = lax.dot_general(wqt_ref[...], cqn, _NT, preferred_element_type=F32) * scale
    for h in range(N_HEADS):
        b = h * QK_DIM
        t1 = qt[b + QK_NOPE:b + QK_NOPE + HALF_ROPE]
        t2 = qt[b + QK_NOPE + HALF_ROPE:b + QK_DIM]
        qt_ref[h, 0:QK_NOPE, :] = qt[b:b + QK_NOPE].astype(BF16)
        qt_ref[h, QK_NOPE:QK_NOPE + HALF_ROPE, :] = (t1 * cos_t - t2 * sin_t).astype(BF16)
        qt_ref[h, QK_NOPE + HALF_ROPE:QK_DIM, :] = (t1 * sin_t + t2 * cos_t).astype(BF16)

    kr_t = kr_ref[...].T
    k1 = kr_t[0:HALF_ROPE]
    k2 = kr_t[HALF_ROPE:QK_ROPE]
    krot_t = jnp.concatenate(
        [k1 * cos_t - k2 * sin_t, k1 * sin_t + k2 * cos_t,
         jnp.zeros((LANES - QK_ROPE, tm), F32)], axis=0)
    krot = krot_t.T.astype(BF16)

    kn = jnp.dot(ckvn, wk_ref[...], preferred_element_type=F32)
    vt = lax.dot_general(wvt_ref[...], ckvn, _NT, preferred_element_type=F32)
    for h in range(N_HEADS):
        k_ref[h, :, 0:QK_NOPE] = kn[:, h * QK_NOPE:(h + 1) * QK_NOPE].astype(BF16)
        k_ref[h, :, QK_NOPE:QK_DIM] = krot[:, :QK_ROPE]
        vt_ref[h, 0:V_HEAD, :] = vt[h * V_HEAD:(h + 1) * V_HEAD].astype(BF16)
        vt_ref[h, V_HEAD:V_ROWS, :] = jnp.ones((V_ROWS - V_HEAD, tm), BF16)


def _qkv(c_lat, pos_row, inv_freq, gq, gkv, wqt, wk, wvt, tm):
    s = c_lat.shape[0]
    scale = math.log2(math.e) / math.sqrt(QK_NOPE + QK_ROPE)
    assert Q_LORA == KV_LORA and (Q_LORA + KV_LORA) % LANES == 0
    return pl.pallas_call(
        functools.partial(_qkv_kernel, scale=scale),
        out_shape=(jax.ShapeDtypeStruct((N_HEADS, QK_DIM, s), BF16),
                   jax.ShapeDtypeStruct((N_HEADS, s, QK_DIM), BF16),
                   jax.ShapeDtypeStruct((N_HEADS, V_ROWS, s), BF16)),
        grid=(s // tm,),
        in_specs=[pl.BlockSpec((tm, Q_LORA), lambda i: (i, 0)),
                  pl.BlockSpec((tm, KV_LORA), lambda i: (i, 1)),
                  pl.BlockSpec((tm, LANES), lambda i: (i, (Q_LORA + KV_LORA) // LANES)),
                  pl.BlockSpec((1, tm), lambda i: (0, i)),
                  _resident((HALF_ROPE, 1), lambda i: (0, 0)),
                  _resident((1, Q_LORA), lambda i: (0, 0)),
                  _resident((1, KV_LORA), lambda i: (0, 0)),
                  _resident(wqt.shape, lambda i: (0, 0)),
                  _resident(wk.shape, lambda i: (0, 0)),
                  _resident(wvt.shape, lambda i: (0, 0))],
        out_specs=(pl.BlockSpec((N_HEADS, QK_DIM, tm), lambda i: (0, 0, i)),
                   pl.BlockSpec((N_HEADS, tm, QK_DIM), lambda i: (0, i, 0)),
                   pl.BlockSpec((N_HEADS, V_ROWS, tm), lambda i: (0, 0, i))),
        name="qkv", compiler_params=_cparams(("parallel",)),
    )(c_lat, c_lat, c_lat, pos_row, inv_freq, gq, gkv, wqt, wk, wvt)


_NEG = -1e30


ATTN_UNROLL = 16


def _attn_kernel(qt_ref, k_ref, vt_ref, o_ref, s_ref, mx_ref, acc_ref, m_ref, *, t):
    seq = k_ref.shape[1]
    tq = 2 * t
    krow = lax.broadcasted_iota(jnp.int32, (t, t), 0)
    qcol = lax.broadcasted_iota(jnp.int32, (t, t), 1)
    causal = krow <= qcol

    def pv(ks, p):
        return jnp.dot(vt_ref[0, :, pl.ds(ks, t)], p, preferred_element_type=F32)

    def put_scores(slot, ks, q_start):
        s = jnp.dot(k_ref[0, pl.ds(ks, t), :], qt_ref[0, :, pl.ds(q_start, tq)],
                    preferred_element_type=F32)
        s_ref[slot] = s
        mx_ref[slot] = jnp.max(s, axis=0, keepdims=True)

    def query_tile(i, carry):
        kd = pl.multiple_of(i * tq, tq)
        n_full = 2 * i

        def qk(ks, q0, nq):
            return jnp.dot(k_ref[0, pl.ds(ks, t), :], qt_ref[0, :, pl.ds(kd + q0, nq)],
                           preferred_element_type=F32)

        m_ref[...] = jnp.full_like(m_ref, _NEG)
        acc_ref[...] = jnp.zeros_like(acc_ref)

        def item(slot, j):
            put_scores(1 - slot, pl.multiple_of((j + 1) * t, t), kd)
            m_old = m_ref[...]
            m_new = jnp.maximum(m_old, mx_ref[slot])
            p = jnp.exp2(s_ref[slot] - m_new).astype(BF16)
            acc_ref[...] = jnp.exp2(m_old - m_new) * acc_ref[...] + pv(pl.multiple_of(j * t, t), p)
            m_ref[...] = m_new

        def run(width, j0):
            for u in range(width):
                item(u % 2, j0 + u)

        done = 0
        width = 2
        while width < ATTN_UNROLL:
            bit = lax.rem(n_full // width, 2)
            pl.when(bit == 1)(functools.partial(run, width, done))
            done = done + bit * width
            width *= 2

        def main(r, c):
            run(ATTN_UNROLL, done + ATTN_UNROLL * r)
            return c

        lax.fori_loop(0, n_full // ATTN_UNROLL, main, 0)

        s1 = jnp.where(causal, qk(kd + t, t, t), _NEG)
        s0 = s_ref[0]
        s0 = jnp.concatenate([jnp.where(causal, s0[:, :t], _NEG), s0[:, t:]], axis=1)
        put_scores(0, 0, pl.multiple_of(jnp.minimum(kd + tq, seq - tq), tq))
        m_old = m_ref[...]
        m0 = jnp.maximum(m_old, jnp.max(s0, axis=0, keepdims=True))
        acc0 = jnp.exp2(m_old - m0) * acc_ref[...] + pv(kd, jnp.exp2(s0 - m0).astype(BF16))
        m0r = m0[:, t:]
        m1 = jnp.maximum(m0r, jnp.max(s1, axis=0, keepdims=True))
        acc1 = jnp.exp2(m0r - m1) * acc0[:, t:] + pv(kd + t, jnp.exp2(s1 - m1).astype(BF16))
        acc = jnp.concatenate([acc0[:, :t], acc1], axis=1)
        o_ref[pl.ds(kd, tq), :] = (acc[:V_HEAD] * (1.0 / acc[V_HEAD:V_HEAD + 1])).T.astype(BF16)
        return carry

    put_scores(0, 0, 0)
    lax.fori_loop(0, seq // tq, query_tile, 0)


def _attention(qt, k, vt, t):
    s = k.shape[1]
    head = lambda h: (h, 0, 0)
    return pl.pallas_call(
        functools.partial(_attn_kernel, t=t),
        out_shape=jax.ShapeDtypeStruct((s, N_HEADS * V_HEAD), BF16),
        grid=(N_HEADS,),
        in_specs=[pl.BlockSpec((1, QK_DIM, s), head),
                  pl.BlockSpec((1, s, QK_DIM), head),
                  pl.BlockSpec((1, V_ROWS, s), head)],
        out_specs=pl.BlockSpec((s, V_HEAD), lambda h: (0, h), pipeline_mode=pl.Buffered(1)),
        scratch_shapes=[pltpu.VMEM((2, t, 2 * t), F32), pltpu.VMEM((2, 1, 2 * t), F32),
                        pltpu.VMEM((V_ROWS, 2 * t), F32), pltpu.VMEM((1, 2 * t), F32)],
        name="attn", compiler_params=_cparams(("parallel",)),
    )(qt, k, vt)


def _layer_norm(r, g, b):
    mu = jnp.mean(r, axis=-1, keepdims=True)
    c = r - mu
    var = jnp.mean(c * c, axis=-1, keepdims=True)
    return c * lax.rsqrt(var + LN_EPS) * g + b


def _outproj_kernel(sg_ref, gm_ref, mla_ref, x_ref, w_ref, g_ref, b_ref, h_ref, hb_ref):
    merged = sg_ref[...].astype(F32) + gm_ref[...].astype(F32) * mla_ref[...].astype(F32)
    mix = jnp.dot(merged.astype(BF16), w_ref[...], preferred_element_type=F32)
    h = _layer_norm(DEEPNORM_ALPHA * x_ref[...] + mix, g_ref[...], b_ref[...])
    h_ref[...] = h
    hb_ref[...] = h.astype(BF16)


def _outproj(ssm_gated, gates, mla, x2, w_out, ln_g, ln_b, tm):
    s = x2.shape[0]
    row = lambda i: (i, 0)
    return pl.pallas_call(
        _outproj_kernel,
        out_shape=(jax.ShapeDtypeStruct((s, D_MODEL), F32),
                   jax.ShapeDtypeStruct((s, D_MODEL), BF16)),
        grid=(s // tm,),
        in_specs=[pl.BlockSpec((tm, D_MODEL), row),
                  pl.BlockSpec((tm, D_MODEL), lambda i: (i, 1)),
                  pl.BlockSpec((tm, D_MODEL), row),
                  pl.BlockSpec((tm, D_MODEL), row),
                  _resident(w_out.shape, lambda i: (0, 0)),
                  _resident((1, D_MODEL), lambda i: (0, 0)),
                  _resident((1, D_MODEL), lambda i: (0, 0))],
        out_specs=(pl.BlockSpec((tm, D_MODEL), row), pl.BlockSpec((tm, D_MODEL), row)),
        name="outproj", compiler_params=_cparams(("parallel",)),
    )(ssm_gated, gates, mla, x2, w_out, ln_g, ln_b)


FFN_RESIDUAL_CHUNKS = 8


def _ffn_kernel(hb_ref, h_ref, wg_ref, wu_ref, wd_ref, g_ref, b_ref, o_ref):
    j = pl.program_id(1)
    chunk = h_ref.shape[0]

    @pl.when(j == 0)
    def _():
        o_ref[...] = jnp.zeros_like(o_ref)

    hb = hb_ref[...]
    gate = jnp.dot(hb, wg_ref[...], preferred_element_type=F32)
    up = jnp.dot(hb, wu_ref[...], preferred_element_type=F32)
    act = (gate * jax.nn.sigmoid(gate) * up).astype(BF16)
    o_ref[...] += jnp.dot(act, wd_ref[...], preferred_element_type=F32)

    @pl.when(j < FFN_RESIDUAL_CHUNKS)
    def _():
        rows = pl.ds(pl.multiple_of(j * chunk, chunk), chunk)
        o_ref[rows, :] += DEEPNORM_ALPHA * h_ref[...]

    @pl.when(j == pl.num_programs(1) - 1)
    def _():
        o_ref[...] = _layer_norm(o_ref[...], g_ref[...], b_ref[...])


def _ffn(hb, h1, wg, wu, wd, ln_g, ln_b, tm, tf):
    s = h1.shape[0]
    dff = wg.shape[1]
    steps = dff // tf
    assert steps >= FFN_RESIDUAL_CHUNKS and tm % FFN_RESIDUAL_CHUNKS == 0
    chunk = tm // FFN_RESIDUAL_CHUNKS
    return pl.pallas_call(
        _ffn_kernel,
        out_shape=jax.ShapeDtypeStruct((s, D_MODEL), F32),
        grid=(s // tm, steps),
        in_specs=[pl.BlockSpec((tm, D_MODEL), lambda i, j: (i, 0)),
                  pl.BlockSpec((chunk, D_MODEL),
                               lambda i, j: (i * FFN_RESIDUAL_CHUNKS
                                             + jnp.minimum(j, FFN_RESIDUAL_CHUNKS - 1), 0)),
                  pl.BlockSpec((D_MODEL, tf), lambda i, j: (0, j)),
                  pl.BlockSpec((D_MODEL, tf), lambda i, j: (0, j)),
                  pl.BlockSpec((tf, D_MODEL), lambda i, j: (j, 0)),
                  _resident((1, D_MODEL), lambda i, j: (0, 0)),
                  _resident((1, D_MODEL), lambda i, j: (0, 0))],
        out_specs=pl.BlockSpec((tm, D_MODEL), lambda i, j: (i, 0)),
        name="ffn", compiler_params=_cparams(("parallel", "arbitrary")),
    )(hb, h1, wg, wu, wd, ln_g, ln_b)


def _tile(s, want):
    t = min(s, want)
    assert s % t == 0, (s, t)
    return t


def kernel(x, positions, w_in, ssm_lambda_re, ssm_lambda_im, ssm_log_dt, ssm_b_re, ssm_b_im,
           ssm_c_re, ssm_c_im, ssm_d, w_glu, q_norm_g, w_uq, kv_norm_g, w_ukv, w_out,
           ln1_g, ln1_b, w_ffn_gate, w_ffn_up, w_ffn_down, ln2_g, ln2_b):
    bsz, seq, d_model = x.shape
    assert bsz == 1 and d_model == D_MODEL and w_in.shape[0] == DEPTH
    x2 = x.reshape(seq, D_MODEL)
    pos_row = positions.reshape(1, seq)
    inv_freq = (1.0 / (ROPE_THETA ** (jnp.arange(0, QK_ROPE, 2, dtype=F32) / QK_ROPE))
                ).reshape(HALF_ROPE, 1)
    h = x2
    for l in range(DEPTH):
        lat_hi = SSM_WIDTH + Q_LORA + KV_LORA + QK_ROPE
        w_a = jnp.pad(w_in[l][:, :lat_hi], ((0, 0), (0, LANES - QK_ROPE))).astype(BF16)
        w_g = w_in[l][:, lat_hi:].astype(BF16)
        wqt = w_uq[l].T.astype(BF16)
        wkv = w_ukv[l].reshape(KV_LORA, N_HEADS, QK_NOPE + V_HEAD)
        wk = wkv[:, :, :QK_NOPE].reshape(KV_LORA, N_HEADS * QK_NOPE).astype(BF16)
        wvt = wkv[:, :, QK_NOPE:].reshape(KV_LORA, N_HEADS * V_HEAD).T.astype(BF16)

        u_slabs, c_lat = _inproj_a(h, w_a, _tile(seq, ROWS_INPROJ))
        gates = _inproj_g(h, w_g, _tile(seq, ROWS_INPROJ))

        m_op, p_op, q_op, a_tab = _ssm_prep(ssm_lambda_re[l], ssm_lambda_im[l], ssm_log_dt[l],
                                            ssm_b_re[l], ssm_b_im[l], ssm_c_re[l], ssm_c_im[l])
        d_tab = jnp.tile(ssm_d[l].reshape(N_SLABS, 1, LANES), (1, 1, SSM_T))
        y_slabs = _ssm_scan(u_slabs, m_op, p_op, q_op, a_tab, d_tab, _tile(seq, ROWS_SSM))
        ssm_gated = _glu(y_slabs, w_glu[l].astype(BF16), gates, _tile(seq, ROWS_GLU))

        qt, k, vt = _qkv(c_lat, pos_row, inv_freq, q_norm_g[l].reshape(1, Q_LORA),
                         kv_norm_g[l].reshape(1, KV_LORA), wqt, wk, wvt, _tile(seq, ROWS_QKV))
        mla = _attention(qt, k, vt, _tile(seq // 2, ROWS_ATTN))

        h, hb = _outproj(ssm_gated, gates, mla, h, w_out[l].astype(BF16),
                         ln1_g[l].reshape(1, D_MODEL), ln1_b[l].reshape(1, D_MODEL),
                         _tile(seq, ROWS_OUTPROJ))

        h = _ffn(hb, h, w_ffn_gate[l].astype(BF16), w_ffn_up[l].astype(BF16),
                 w_ffn_down[l].astype(BF16), ln2_g[l].reshape(1, D_MODEL),
                 ln2_b[l].reshape(1, D_MODEL), _tile(seq, ROWS_FFN), COLS_FFN)
    return h.reshape(bsz, seq, D_MODEL)
```
